```python
import functools
import jax, jax.numpy as jnp
from jax import lax
import numpy as np

D_MODEL = 2048
BATCH = 8
SEQ = 2048
DEPTH = 1
DEC_BATCH = 128
DEC_SEQ = 1
PAST_LEN = 2048
PAGE_SIZE = 128

LRU_WIDTH = D_MODEL // 2
LRU_BLOCKS = 8
LRU_BLOCK = LRU_WIDTH // LRU_BLOCKS
LRU_C = 8.0
CONV_W = 4
N_HEADS = 8
HEAD_DIM = (D_MODEL // 2) // N_HEADS
ATT_WIDTH = N_HEADS * HEAD_DIM
MIX_WIDTH = LRU_WIDTH + ATT_WIDTH
Q_BLOCK = 128
SB_BIAS_INIT = -7.0
N_EXPERTS = 64
TOP_K = 8
N_GROUPS = 8
TOPK_GROUPS = 4
EXPERT_FF = D_MODEL // 4
SHARED_FF = D_MODEL // 4
ROUTED_SCALE = 2.5
MOE_BLOCK = 128
EPS = 1e-6

kernel_name = 'hymba_hawk_stickbreak_moe_step'


def rmsnorm(x, g):
    xf = x.astype(jnp.float32)
    y = xf * lax.rsqrt(jnp.mean(xf * xf, axis=-1, keepdims=True) + EPS)
    return (y * g.astype(jnp.float32)).astype(x.dtype)


def adaln(c, w, b):
    m = jax.nn.silu(c) @ w + b
    return jnp.split(m[:, None, :], 6, axis=-1)


def swiglu(x, w1, w3, w2):
    return (jax.nn.silu(x @ w1) * (x @ w3)) @ w2


def causal_conv(x, buf, w, b):
    s = x.shape[1]
    xp = jnp.concatenate([buf.astype(x.dtype), x], axis=1)
    y = sum(xp[:, j:j + s] * w[j] for j in range(CONV_W)) + b
    return y, xp[:, -(CONV_W - 1):]


def rg_lru(x, h0, w_a, b_a, w_x, b_x, lam):
    bsz, s, wid = x.shape
    xf = x.astype(jnp.float32)
    xb = xf.reshape(bsz, s, LRU_BLOCKS, LRU_BLOCK)
    r = jax.nn.sigmoid(jnp.einsum('bsnc,ncd->bsnd', xb, w_a).reshape(bsz, s, wid) + b_a)
    i = jax.nn.sigmoid(jnp.einsum('bsnc,ncd->bsnd', xb, w_x).reshape(bsz, s, wid) + b_x)
    log_a = -LRU_C * r * jax.nn.softplus(-lam.astype(jnp.float32))
    a = jnp.exp(log_a)
    u = jnp.sqrt(-jnp.expm1(2.0 * log_a)) * (i * xf)

    def step(h, au):
        h = au[0] * h + au[1]
        return h, h

    h_last, hs = lax.scan(step, h0.astype(jnp.float32), (a.swapaxes(0, 1), u.swapaxes(0, 1)))
    return hs.swapaxes(0, 1).astype(x.dtype), h_last


def stick_breaking(q, k, v, bias, q_pos, k_pos):
    z = jnp.einsum('bqhd,bkhd->bhqk', q, k, preferred_element_type=jnp.float32) * (HEAD_DIM ** -0.5)
    z = z + bias.astype(jnp.float32)[None, :, None, None]
    causal = k_pos[None, :] < q_pos[:, None]
    log_keep = jnp.where(causal, jax.nn.log_sigmoid(-z), 0.0)
    later = lax.cumsum(log_keep, axis=3, reverse=True) - log_keep
    a = jnp.where(causal, jnp.exp(jax.nn.log_sigmoid(z) + later), 0.0)
    o = jnp.einsum('bhqk,bkhd->bqhd', a, v.astype(jnp.float32))
    return o.astype(q.dtype)


def attend_prompt(q, k, v, bias):
    bsz, s, h, d = q.shape
    nqb = s // Q_BLOCK
    pos = jnp.arange(s)
    qb = q.reshape(bsz, nqb, Q_BLOCK, h, d).swapaxes(0, 1)
    pb = pos.reshape(nqb, Q_BLOCK)
    ob = lax.map(lambda a: stick_breaking(a[0], k, v, bias, a[1], pos), (qb, pb))
    return ob.swapaxes(0, 1).reshape(bsz, s, h, d)


def attend_with_past(q, k, v, bias, past_k, past_v):
    p = past_k.shape[1]
    s = q.shape[1]
    k_all = jnp.concatenate([past_k.astype(k.dtype), k], axis=1)
    v_all = jnp.concatenate([past_v.astype(v.dtype), v], axis=1)
    return stick_breaking(q, k_all, v_all, bias, p + jnp.arange(s), jnp.arange(p + s))


def gather_pages(pool, page_table):
    g = pool[page_table]
    return g.reshape(g.shape[0], g.shape[1] * g.shape[2], g.shape[3], g.shape[4])


def routed_experts(t, idx, wts, w1, w3, w2):
    n_tok, d = t.shape
    n_exp = w1.shape[0]
    n_assign = n_tok * TOP_K
    n_blocks = -(-n_assign // MOE_BLOCK) + n_exp
    n_rows = n_blocks * MOE_BLOCK
    flat_e = idx.reshape(-1)
    flat_tok = jnp.arange(n_assign, dtype=jnp.int32) // TOP_K
    flat_w = wts.reshape(-1)
    order = jnp.argsort(flat_e)
    sorted_e = flat_e[order]
    counts = jnp.bincount(flat_e, length=n_exp)
    padded = (counts + MOE_BLOCK - 1) // MOE_BLOCK * MOE_BLOCK
    pad_end = jnp.cumsum(padded)
    pad_start = pad_end - padded
    start = jnp.cumsum(counts) - counts
    dest = pad_start[sorted_e] + (jnp.arange(n_assign) - start[sorted_e])
    row_tok = jnp.full((n_rows,), n_tok, jnp.int32).at[dest].set(flat_tok[order])
    row_w = jnp.zeros((n_rows,), jnp.float32).at[dest].set(flat_w[order])
    block_e = jnp.minimum(jnp.searchsorted(pad_end, jnp.arange(n_blocks) * MOE_BLOCK, side='right'), n_exp - 1)
    t_pad = jnp.concatenate([t, jnp.zeros((1, d), t.dtype)], axis=0)

    def body(out, blk):
        tok, wr, e = blk
        ye = swiglu(t_pad[tok], w1[e], w3[e], w2[e])
        return out.at[tok].add(ye.astype(jnp.float32) * wr[:, None]), None

    out, _ = lax.scan(body, jnp.zeros((n_tok + 1, d), jnp.float32),
                      (row_tok.reshape(n_blocks, MOE_BLOCK), row_w.reshape(n_blocks, MOE_BLOCK), block_e))
    return out[:n_tok]


def moe(h, w_router, b_router, w1_e, w3_e, w2_e, w1_s, w3_s, w2_s):
    bsz, s, d = h.shape
    t = h.reshape(bsz * s, d)
    scores = jax.nn.sigmoid(jnp.einsum('td,de->te', t, w_router, preferred_element_type=jnp.float32))
    biased = scores + b_router.astype(jnp.float32)
    grp = biased.reshape(-1, N_GROUPS, N_EXPERTS // N_GROUPS)
    grp_score = lax.top_k(grp, 2)[0].sum(-1)
    _, top_g = lax.top_k(grp_score, TOPK_GROUPS)
    g_mask = jax.nn.one_hot(top_g, N_GROUPS, dtype=jnp.float32).sum(1) > 0
    e_mask = jnp.repeat(g_mask, N_EXPERTS // N_GROUPS, axis=1)
    _, idx = lax.top_k(jnp.where(e_mask, biased, -jnp.inf), TOP_K)
    wts = jnp.take_along_axis(scores, idx, axis=1)
    wts = wts / jnp.sum(wts, axis=-1, keepdims=True) * ROUTED_SCALE
    routed = routed_experts(t, idx, wts, w1_e, w3_e, w2_e)
    shared = swiglu(t, w1_s, w3_s, w2_s).astype(jnp.float32)
    return (routed + shared).astype(h.dtype).reshape(bsz, s, d)


def decoder_layer(x, c, conv_buf, h0, attend, lp):
    sh1, sc1, g1, sh2, sc2, g2 = adaln(c, lp['w_mod'], lp['b_mod'])
    bsz, s, _ = x.shape
    h = rmsnorm(x, lp['g_norm1']) * (1.0 + sc1) + sh1
    proj = h @ lp['w_in']
    splits = [LRU_WIDTH, 2 * LRU_WIDTH, 2 * LRU_WIDTH + ATT_WIDTH, 2 * LRU_WIDTH + 2 * ATT_WIDTH]
    x_lru, gate_lru, q, k, v = jnp.split(proj, splits, axis=-1)
    xc, new_buf = causal_conv(x_lru, conv_buf, lp['conv_w'], lp['conv_b'])
    y_lru, h_last = rg_lru(xc, h0, lp['w_gate_a'], lp['b_gate_a'], lp['w_gate_x'], lp['b_gate_x'], lp['lru_lambda'])
    y_lru = jax.nn.gelu(gate_lru) * y_lru
    q = q.reshape(bsz, s, N_HEADS, HEAD_DIM)
    k = k.reshape(bsz, s, N_HEADS, HEAD_DIM)
    v = v.reshape(bsz, s, N_HEADS, HEAD_DIM)
    y_att = attend(q, k, v, lp['sb_bias']).reshape(bsz, s, ATT_WIDTH)
    mixed = jnp.concatenate([rmsnorm(y_lru, lp['g_out_lru']), rmsnorm(y_att, lp['g_out_att'])], axis=-1) @ lp['w_out']
    x = x + g1 * mixed
    h = rmsnorm(x, lp['g_norm2']) * (1.0 + sc2) + sh2
    x = x + g2 * moe(h, lp['w_router'], lp['b_router'], lp['w1_e'], lp['w3_e'], lp['w2_e'],
                     lp['w1_s'], lp['w3_s'], lp['w2_s'])
    return x, k, v, h_last, new_buf


def setup_inputs(seed: int = 0) -> dict:
    key = jax.random.key(seed)
    ks = jax.random.split(key, 40)

    def nrm(k, shape, s):
        return jax.random.normal(k, shape, jnp.float32) * s

    D = D_MODEL
    L = DEPTH
    n_pages = PAST_LEN // PAGE_SIZE
    n_used = DEC_BATCH * n_pages
    n_pool = n_used + max(1, n_used // 4)
    page_table = jax.random.permutation(ks[0], n_pool)[:n_used].reshape(DEC_BATCH, n_pages).astype(jnp.int32)
    u = jax.random.uniform(ks[20], (L, LRU_WIDTH), jnp.float32, 0.9, 0.999)
    lru_lambda = -jnp.log(jnp.expm1(-jnp.log(u) / LRU_C))
    return {
        'x_prompt': nrm(ks[1], (BATCH, SEQ, D), 1.0),
        'x_sample': nrm(ks[2], (DEC_BATCH, DEC_SEQ, D), 1.0),
        'c_prompt': nrm(ks[3], (BATCH, D), 1.0),
        'c_sample': nrm(ks[4], (DEC_BATCH, D), 1.0),
        'cache_k': nrm(ks[5], (L, n_pool, PAGE_SIZE, N_HEADS, HEAD_DIM), 1.0),
        'cache_v': nrm(ks[6], (L, n_pool, PAGE_SIZE, N_HEADS, HEAD_DIM), 1.0),
        'page_table': page_table,
        'state_lru_h': nrm(ks[7], (L, DEC_BATCH, LRU_WIDTH), 0.5),
        'state_conv': nrm(ks[8], (L, DEC_BATCH, CONV_W - 1, LRU_WIDTH), 1.0),
        'w_mod': nrm(ks[9], (L, D, 6 * D), 0.5 * D ** -0.5),
        'b_mod': nrm(ks[10], (L, 6 * D), 0.01),
        'g_norm1': 1.0 + nrm(ks[11], (L, D), 0.02),
        'g_norm2': 1.0 + nrm(ks[12], (L, D), 0.02),
        'w_in': nrm(ks[13], (L, D, 2 * LRU_WIDTH + 3 * ATT_WIDTH), D ** -0.5),
        'conv_w': nrm(ks[14], (L, CONV_W, LRU_WIDTH), CONV_W ** -0.5),
        'conv_b': nrm(ks[15], (L, LRU_WIDTH), 0.01),
        'w_gate_a': nrm(ks[16], (L, LRU_BLOCKS, LRU_BLOCK, LRU_BLOCK), LRU_BLOCK ** -0.5),
        'b_gate_a': nrm(ks[17], (L, LRU_WIDTH), 0.01),
        'w_gate_x': nrm(ks[18], (L, LRU_BLOCKS, LRU_BLOCK, LRU_BLOCK), LRU_BLOCK ** -0.5),
        'b_gate_x': nrm(ks[19], (L, LRU_WIDTH), 0.01),
        'lru_lambda': lru_lambda,
        'sb_bias': SB_BIAS_INIT + nrm(ks[33], (L, N_HEADS), 0.5),
        'g_out_lru': 1.0 + nrm(ks[21], (L, LRU_WIDTH), 0.02),
        'g_out_att': 1.0 + nrm(ks[22], (L, ATT_WIDTH), 0.02),
        'w_out': nrm(ks[23], (L, MIX_WIDTH, D), MIX_WIDTH ** -0.5),
        'w_router': nrm(ks[24], (L, D, N_EXPERTS), D ** -0.5),
        'b_router': nrm(ks[25], (L, N_EXPERTS), 0.01),
        'w1_e': nrm(ks[26], (L, N_EXPERTS, D, EXPERT_FF), D ** -0.5),
        'w3_e': nrm(ks[27], (L, N_EXPERTS, D, EXPERT_FF), D ** -0.5),
        'w2_e': nrm(ks[28], (L, N_EXPERTS, EXPERT_FF, D), EXPERT_FF ** -0.5),
        'w1_s': nrm(ks[29], (L, D, SHARED_FF), D ** -0.5),
        'w3_s': nrm(ks[30], (L, D, SHARED_FF), D ** -0.5),
        'w2_s': nrm(ks[31], (L, SHARED_FF, D), SHARED_FF ** -0.5),
        'g_final': 1.0 + nrm(ks[32], (D,), 0.02),
    }


def reference(x_prompt, x_sample, c_prompt, c_sample, cache_k, cache_v, page_table, state_lru_h, state_conv,
              w_mod, b_mod, g_norm1, g_norm2, w_in, conv_w, conv_b, w_gate_a, b_gate_a, w_gate_x, b_gate_x,
              lru_lambda, sb_bias, g_out_lru, g_out_att, w_out, w_router, b_router, w1_e, w3_e, w2_e,
              w1_s, w3_s, w2_s, g_final):
    xp, xs = x_prompt, x_sample
    kp_l, vp_l, hp_l, bp_l = [], [], [], []
    ks_l, vs_l, hs_l, bs_l = [], [], [], []
    for l in range(DEPTH):
        lp = {'w_mod': w_mod[l], 'b_mod': b_mod[l], 'g_norm1': g_norm1[l], 'g_norm2': g_norm2[l],
              'w_in': w_in[l], 'conv_w': conv_w[l], 'conv_b': conv_b[l],
              'w_gate_a': w_gate_a[l], 'b_gate_a': b_gate_a[l], 'w_gate_x': w_gate_x[l], 'b_gate_x': b_gate_x[l],
              'lru_lambda': lru_lambda[l], 'sb_bias': sb_bias[l],
              'g_out_lru': g_out_lru[l], 'g_out_att': g_out_att[l],
              'w_out': w_out[l], 'w_router': w_router[l], 'b_router': b_router[l],
              'w1_e': w1_e[l], 'w3_e': w3_e[l], 'w2_e': w2_e[l],
              'w1_s': w1_s[l], 'w3_s': w3_s[l], 'w2_s': w2_s[l]}
        bsz = xp.shape[0]
        xp, kp, vp, hp, bp = decoder_layer(
            xp, c_prompt, jnp.zeros((bsz, CONV_W - 1, LRU_WIDTH), xp.dtype),
            jnp.zeros((bsz, LRU_WIDTH), jnp.float32), attend_prompt, lp)
        attend_s = functools.partial(attend_with_past, past_k=gather_pages(cache_k[l], page_table),
                                     past_v=gather_pages(cache_v[l], page_table))
        xs, k_s, v_s, h_s, b_s = decoder_layer(xs, c_sample, state_conv[l], state_lru_h[l], attend_s, lp)
        kp_l.append(kp); vp_l.append(vp); hp_l.append(hp); bp_l.append(bp)
        ks_l.append(k_s); vs_l.append(v_s); hs_l.append(h_s); bs_l.append(b_s)
    y_prompt = rmsnorm(xp, g_final)
    y_sample = rmsnorm(xs, g_final)
    return (y_prompt, y_sample,
            jnp.stack(kp_l), jnp.stack(vp_l), jnp.stack(hp_l), jnp.stack(bp_l),
            jnp.stack(ks_l), jnp.stack(vs_l), jnp.stack(hs_l), jnp.stack(bs_l))
```

```python
import functools

import jax
import jax.numpy as jnp
from jax import lax
from jax.experimental import pallas as pl
from jax.experimental.pallas import tpu as pltpu

F32 = jnp.float32
BF16 = jnp.bfloat16

EPS = 1e-6
LRU_C = 8.0
CONV_W = 4
LRU_BLOCKS = 8
N_GROUPS = 8
TOPK_GROUPS = 4
TOP_K = 8
ROUTED_SCALE = 2.5
LANES = 128
HEAD_DIM = 128
MOE_BM = 256
VMEM_LIMIT = 56 * 1024 * 1024


def _cparams(sem):
    return pltpu.CompilerParams(dimension_semantics=sem, vmem_limit_bytes=VMEM_LIMIT)


def _dot(a, b):
    return jnp.dot(a, b, preferred_element_type=F32)


def _dot_nt(a, b):
    return lax.dot_general(a, b, (((1,), (1,)), ((), ())), preferred_element_type=F32)


def _split(x):
    hi = x.astype(BF16)
    lo = (x - hi.astype(F32)).astype(BF16)
    return hi, lo


def _softplus(z):
    return jnp.maximum(z, 0.0) + jnp.log(1.0 + jnp.exp(-jnp.abs(z)))


def _sigmoid(z):
    return 1.0 / (1.0 + jnp.exp(-z))


def _silu(z):
    return z * _sigmoid(z)


def _gelu_tanh(x):
    return 0.5 * x * (1.0 + jnp.tanh(0.7978845608028654 * (x + 0.044715 * (x * x * x))))


def _rms(x, g):
    return x * lax.rsqrt(jnp.mean(x * x, axis=-1, keepdims=True) + EPS) * g


def _mod_kernel(c_ref, w_ref, b_ref, o_ref):
    a = _silu(c_ref[...]).astype(BF16)
    o_ref[...] = _dot(a, w_ref[...].astype(BF16)) + b_ref[...]


def _mod_call(c_all, w_mod, b_mod, tn=512):
    m, d = c_all.shape
    n = w_mod.shape[1]
    return pl.pallas_call(
        _mod_kernel,
        grid=(n // tn,),
        in_specs=[pl.BlockSpec((m, d), lambda j: (0, 0)),
                  pl.BlockSpec((d, tn), lambda j: (0, j)),
                  pl.BlockSpec((1, tn), lambda j: (0, j))],
        out_specs=pl.BlockSpec((m, tn), lambda j: (0, j)),
        out_shape=jax.ShapeDtypeStruct((m, n), F32),
        compiler_params=_cparams(("arbitrary",)),
        name="mod",
    )(c_all, w_mod, b_mod)


def _mod_specs(mod, tm, d, rows_per_mod, n_grid_axes):
    if mod.ndim == 3:
        tiles = rows_per_mod // tm
        if n_grid_axes == 1:
            return pl.BlockSpec((None, 1, d), lambda i: (i // tiles, 0, 0))
        return pl.BlockSpec((None, 1, d), lambda i, j: (i // tiles, 0, 0))
    if n_grid_axes == 1:
        return pl.BlockSpec((tm, d), lambda i: (i, 0))
    return pl.BlockSpec((tm, d), lambda i, j: (i, 0))


def _in_proj_kernel(x_ref, g_ref, sc_ref, sh_ref, w_ref, o_ref, h_scr):
    @pl.when(pl.program_id(1) == 0)
    def _():
        h = _rms(x_ref[...], g_ref[...]) * (1.0 + sc_ref[...]) + sh_ref[...]
        h_scr[...] = h.astype(BF16)

    o_ref[...] = _dot(h_scr[...], w_ref[...])


def _in_proj_call(x, g, sc, sh, w_in, rows_per_mod, tm):
    t, d = x.shape
    w = w_in.shape[1] // 5
    return pl.pallas_call(
        _in_proj_kernel,
        grid=(t // tm, 5),
        in_specs=[pl.BlockSpec((tm, d), lambda i, j: (i, 0)),
                  pl.BlockSpec((1, d), lambda i, j: (0, 0)),
                  _mod_specs(sc, tm, d, rows_per_mod, 2),
                  _mod_specs(sh, tm, d, rows_per_mod, 2),
                  pl.BlockSpec((d, w), lambda i, j: (0, j))],
        out_specs=pl.BlockSpec((None, tm, w), lambda i, j: (j, i, 0)),
        out_shape=jax.ShapeDtypeStruct((5, t, w), F32),
        scratch_shapes=[pltpu.VMEM((tm, d), BF16)],
        compiler_params=_cparams(("arbitrary", "arbitrary")),
        name="in_proj",
    )(x, g, sc, sh, w_in)


def _lru_gates(xc, wa_ref, ba, wx_ref, bx, sp):
    xb = xc.astype(BF16)
    blk = xc.shape[1] // LRU_BLOCKS
    za = jnp.concatenate([_dot(xb[:, n * blk:(n + 1) * blk], wa_ref[n]) for n in range(LRU_BLOCKS)], axis=1)
    zx = jnp.concatenate([_dot(xb[:, n * blk:(n + 1) * blk], wx_ref[n]) for n in range(LRU_BLOCKS)], axis=1)
    r = _sigmoid(za + ba)
    i = _sigmoid(zx + bx)
    log_a = -LRU_C * r * sp
    a = jnp.exp(log_a)
    u = jnp.sqrt(1.0 - jnp.exp(2.0 * log_a)) * (i * xc)
    return a, u


def _lru_prompt_kernel(xl_ref, gt_ref, cw_ref, cb_ref, wa_ref, ba_ref, wx_ref, bx_ref, lam_ref,
                       y_ref, hl_ref, xp_scr, h_scr, a_scr, u_scr):
    nb, tc, w = xl_ref.shape
    pad = xp_scr.shape[1] - tc
    nc = w // LANES

    @pl.when(pl.program_id(0) == 0)
    def _():
        xp_scr[:, 0:pad, :] = jnp.zeros((nb, pad, w), F32)
        h_scr[...] = jnp.zeros_like(h_scr)

    sp = _softplus(-lam_ref[...])
    cw = cw_ref[...]
    for b in range(nb):
        x = xl_ref[b]
        xp_scr[b, pad:pad + tc, :] = x
        xc = cw[CONV_W - 1:CONV_W, :] * x + cb_ref[...]
        for j in range(CONV_W - 1):
            off = pad - (CONV_W - 1) + j
            xc = xc + cw[j:j + 1, :] * xp_scr[b, off:off + tc, :]
        a, u = _lru_gates(xc, wa_ref, ba_ref[...], wx_ref, bx_ref[...], sp)
        for c in range(nc):
            a_scr[c, b * tc:(b + 1) * tc, :] = a[:, c * LANES:(c + 1) * LANES]
            u_scr[c, b * tc:(b + 1) * tc, :] = u[:, c * LANES:(c + 1) * LANES]
        xp_scr[b, 0:pad, :] = x[tc - pad:tc, :]

    h = [h_scr[:, c * LANES:(c + 1) * LANES] for c in range(nc)]
    for t in range(tc):
        rows = pl.ds(t, nb, stride=tc)
        for c in range(nc):
            h[c] = a_scr[c, rows, :] * h[c] + u_scr[c, rows, :]
            u_scr[c, rows, :] = h[c]
    h = jnp.concatenate(h, axis=1)
    h_scr[...] = h
    hl_ref[...] = h

    for b in range(nb):
        hs = jnp.concatenate([u_scr[c, b * tc:(b + 1) * tc, :] for c in range(nc)], axis=1)
        y_ref[b] = (_gelu_tanh(gt_ref[b]) * hs).astype(BF16)


def _lru_prompt_call(proj5, nb, cw, cb, wa, ba, wx, bx, lam, tc=128):
    _, t, w = proj5.shape
    s = t // nb
    p4 = proj5.reshape(5, nb, s, w)
    full = lambda shape: pl.BlockSpec(shape, lambda c: (0,) * len(shape))
    return pl.pallas_call(
        _lru_prompt_kernel,
        grid=(s // tc,),
        in_specs=[pl.BlockSpec((None, nb, tc, w), lambda c: (0, 0, c, 0)),
                  pl.BlockSpec((None, nb, tc, w), lambda c: (1, 0, c, 0)),
                  full(cw.shape), full(cb.shape), full(wa.shape), full(ba.shape),
                  full(wx.shape), full(bx.shape), full(lam.shape)],
        out_specs=[pl.BlockSpec((nb, tc, w), lambda c: (0, c, 0)),
                   pl.BlockSpec((nb, w), lambda c: (0, 0))],
        out_shape=[jax.ShapeDtypeStruct((nb, s, w), BF16),
                   jax.ShapeDtypeStruct((nb, w), F32)],
        scratch_shapes=[pltpu.VMEM((nb, tc + 8, w), F32),
                        pltpu.VMEM((nb, w), F32),
                        pltpu.VMEM((w // LANES, nb * tc, LANES), F32),
                        pltpu.VMEM((w // LANES, nb * tc, LANES), F32)],
        compiler_params=_cparams(("arbitrary",)),
        name="lru_prompt",
    )(p4, p4, cw, cb, wa, ba, wx, bx, lam)


def _lru_sample_kernel(xl_ref, gt_ref, cs_ref, h0_ref, cw_ref, cb_ref, wa_ref, ba_ref, wx_ref, bx_ref,
                       lam_ref, y_ref, hl_ref):
    w = xl_ref.shape[1]
    x = xl_ref[...]
    cw = cw_ref[...]
    xc = cw[CONV_W - 1:CONV_W, :] * x + cb_ref[...]
    for j in range(CONV_W - 1):
        xc = xc + cw[j:j + 1, :] * cs_ref[:, j * w:(j + 1) * w]
    a, u = _lru_gates(xc, wa_ref, ba_ref[...], wx_ref, bx_ref[...], _softplus(-lam_ref[...]))
    h = a * h0_ref[...] + u
    hl_ref[...] = h
    y_ref[...] = (_gelu_tanh(gt_ref[...]) * h).astype(BF16)


def _lru_sample_call(proj5, conv_state, h0, cw, cb, wa, ba, wx, bx, lam):
    _, t, w = proj5.shape
    full = lambda shape: pl.BlockSpec(shape, lambda c: (0,) * len(shape))
    return pl.pallas_call(
        _lru_sample_kernel,
        grid=(1,),
        in_specs=[pl.BlockSpec((None, t, w), lambda c: (0, 0, 0)),
                  pl.BlockSpec((None, t, w), lambda c: (1, 0, 0)),
                  full(conv_state.shape), full(h0.shape),
                  full(cw.shape), full(cb.shape), full(wa.shape), full(ba.shape),
                  full(wx.shape), full(bx.shape), full(lam.shape)],
        out_specs=[full((t, w)), full((t, w))],
        out_shape=[jax.ShapeDtypeStruct((t, w), BF16), jax.ShapeDtypeStruct((t, w), F32)],
        compiler_params=_cparams(("arbitrary",)),
        name="lru_sample",
    )(proj5, proj5, conv_state, h0, cw, cb, wa, ba, wx, bx, lam)


def _strict_upper(n):
    j = lax.broadcasted_iota(jnp.int32, (n, n), 0)
    s = lax.broadcasted_iota(jnp.int32, (n, n), 1)
    return jnp.where(j > s, 1.0, 0.0).astype(BF16)


def _sb_block(q, k, v, bias, upper, r_run, acc, scale, causal):
    z = _dot_nt(q, k) * scale + bias
    sp = _softplus(z)
    lk = -sp
    if causal is not None:
        lk = jnp.where(causal, lk, 0.0)
    hi, lo = _split(lk)
    later = _dot(hi, upper) + _dot(lo, upper)
    p = jnp.exp((z - sp) + later + r_run)
    if causal is not None:
        p = jnp.where(causal, p, 0.0)
    acc = acc + _dot(p.astype(BF16), v)
    r_run = r_run + jnp.sum(lk, axis=1, keepdims=True)
    return r_run, acc


def _attn_prompt_kernel(q_ref, k_ref, v_ref, b_ref, o_ref):
    tq, hd = q_ref.shape
    qi = pl.program_id(2)
    scale = hd ** -0.5
    q = q_ref[...].astype(BF16)
    bias = b_ref[...]
    upper = _strict_upper(tq)
    row = lax.broadcasted_iota(jnp.int32, (tq, tq), 0)
    col = lax.broadcasted_iota(jnp.int32, (tq, tq), 1)

    def kv(kb):
        start = pl.multiple_of(kb * tq, tq)
        return k_ref[pl.ds(start, tq), :].astype(BF16), v_ref[pl.ds(start, tq), :].astype(BF16)

    k, v = kv(qi)
    carry = _sb_block(q, k, v, bias, upper, jnp.zeros((tq, 1), F32), jnp.zeros((tq, hd), F32),
                      scale, col < row)

    def body(n, c):
        k, v = kv(qi - 1 - n)
        return _sb_block(q, k, v, bias, upper, c[0], c[1], scale, None)

    _, acc = lax.fori_loop(0, qi, body, carry)
    o_ref[...] = acc.astype(BF16)


def _attn_prompt_call(proj5, nb, sb_bias, tq=256):
    _, t, w = proj5.shape
    s = t // nb
    nh = w // HEAD_DIM
    nq = s // tq
    bias = jnp.broadcast_to(sb_bias.astype(F32)[:, None, None], (nh, 1, tq))
    return pl.pallas_call(
        _attn_prompt_kernel,
        grid=(nb, nh, nq),
        in_specs=[pl.BlockSpec((None, tq, HEAD_DIM), lambda b, h, i: (2, b * nq + i, h)),
                  pl.BlockSpec((None, s, HEAD_DIM), lambda b, h, i: (3, b, h)),
                  pl.BlockSpec((None, s, HEAD_DIM), lambda b, h, i: (4, b, h)),
                  pl.BlockSpec((None, 1, tq), lambda b, h, i: (h, 0, 0))],
        out_specs=pl.BlockSpec((tq, HEAD_DIM), lambda b, h, i: (b * nq + i, h)),
        out_shape=jax.ShapeDtypeStruct((t, w), BF16),
        compiler_params=_cparams(("arbitrary", "arbitrary", "arbitrary")),
        name="attn_prompt",
    )(proj5, proj5, proj5, bias)


def _attn_sample_kernel(pt_ref, q_ref, b_ref, *refs, pages_per_step):
    k_refs = refs[:pages_per_step]
    v_refs = refs[pages_per_step:2 * pages_per_step]
    o_ref, r_scr, acc_scr = refs[2 * pages_per_step:]
    nh, hd = q_ref.shape
    page = k_refs[0].shape[0] // nh
    step = pl.program_id(1)

    @pl.when(step == 0)
    def _():
        r_scr[...] = jnp.zeros_like(r_scr)
        acc_scr[...] = jnp.zeros_like(acc_scr)

    head = lax.broadcasted_iota(jnp.int32, (nh, hd), 0)
    q = q_ref[...].astype(BF16)
    upper = _strict_upper(page)
    bias = b_ref[...]
    r_run = r_scr[...]
    acc = acc_scr[...]
    for g in range(pages_per_step):
        z = jnp.zeros((nh, page), F32)
        for h in range(nh):
            kh = k_refs[g][pl.ds(h, page, stride=nh), :].astype(BF16)
            z = jnp.where(head == h, _dot_nt(q, kh), z)
        z = z * (hd ** -0.5) + bias
        sp = _softplus(z)
        lk = -sp
        hi, lo = _split(lk)
        later = _dot(hi, upper) + _dot(lo, upper)
        p = jnp.exp((z - sp) + later + r_run).astype(BF16)
        for h in range(nh):
            vh = v_refs[g][pl.ds(h, page, stride=nh), :].astype(BF16)
            acc = acc + jnp.where(head == h, _dot(p, vh), 0.0)
        r_run = r_run + jnp.sum(lk, axis=1, keepdims=True)
    r_scr[...] = r_run
    acc_scr[...] = acc

    @pl.when(step == pl.num_programs(1) - 1)
    def _():
        o_ref[...] = acc.astype(BF16)


def _attn_sample_call(q, cache_k, cache_v, page_table, sb_bias, pages_per_step=4):
    db, w = q.shape
    n_pool, page, nh, hd = cache_k.shape
    n_pages = page_table.shape[1]
    ck = cache_k.reshape(n_pool, page * nh, hd)
    cv = cache_v.reshape(n_pool, page * nh, hd)
    bias = jnp.broadcast_to(sb_bias.astype(F32)[:, None], (nh, page))

    def page_spec(g):
        return pl.BlockSpec((None, page * nh, hd),
                            lambda b, s, pt: (pt[b, n_pages - 1 - (s * pages_per_step + g)], 0, 0))

    specs = [page_spec(g) for g in range(pages_per_step)]
    return pl.pallas_call(
        functools.partial(_attn_sample_kernel, pages_per_step=pages_per_step),
        grid_spec=pltpu.PrefetchScalarGridSpec(
            num_scalar_prefetch=1,
            grid=(db, n_pages // pages_per_step),
            in_specs=[pl.BlockSpec((None, nh, hd), lambda b, s, pt: (b, 0, 0)),
                      pl.BlockSpec((nh, page), lambda b, s, pt: (0, 0))] + specs + specs,
            out_specs=pl.BlockSpec((None, nh, hd), lambda b, s, pt: (b, 0, 0)),
            scratch_shapes=[pltpu.VMEM((nh, 1), F32), pltpu.VMEM((nh, hd), F32)]),
        out_shape=jax.ShapeDtypeStruct((db, nh, hd), BF16),
        compiler_params=_cparams(("arbitrary", "arbitrary")),
        name="attn_sample",
    )(page_table, q.reshape(db, nh, hd), bias, *([ck] * pages_per_step), *([cv] * pages_per_step)).reshape(db, w)


def _xor_partner(x, lane, s):
    n = x.shape[1]
    up = pltpu.roll(x, n - s, axis=1)
    dn = pltpu.roll(x, s, axis=1)
    return jnp.where((lane & s) == 0, up, dn)


def _route(h2, wr_hi_ref, wr_lo_ref, br):
    tm = h2.shape[0]
    n_exp = LANES // 2
    per_group = n_exp // N_GROUPS
    h_hi, h_lo = _split(h2)
    logits = _dot(h_hi, wr_hi_ref[...]) + _dot(h_lo, wr_hi_ref[...]) + _dot(h_hi, wr_lo_ref[...])
    scores = _sigmoid(logits)
    biased = scores + br
    lane = lax.broadcasted_iota(jnp.int32, (tm, LANES), 1)
    e = lane & (n_exp - 1)
    g = e >> (per_group.bit_length() - 1)
    m1 = biased
    m2 = jnp.full_like(biased, -jnp.inf)
    for s in (1, 2, 4):
        p1 = _xor_partner(m1, lane, s)
        p2 = _xor_partner(m2, lane, s)
        m1, m2 = jnp.maximum(m1, p1), jnp.maximum(jnp.minimum(m1, p1), jnp.maximum(m2, p2))
    gs = m1 + m2
    g_rank = jnp.zeros((tm, LANES), jnp.int32)
    for k in range(1, N_GROUPS):
        other = pltpu.roll(gs, per_group * k, axis=1)
        beats = (other > gs) | ((other == gs) & (g >= k))
        g_rank = g_rank + beats.astype(jnp.int32)
    masked = jnp.where(g_rank < TOPK_GROUPS, biased, -jnp.inf)
    e_rank = jnp.zeros((tm, LANES), jnp.int32)
    for k in range(1, n_exp):
        other = pltpu.roll(masked, k, axis=1)
        beats = (other > masked) | ((other == masked) & (e >= k))
        e_rank = e_rank + beats.astype(jnp.int32)
    first = lane < n_exp
    sel = (e_rank < TOP_K) & first
    wsel = jnp.where(sel, scores, 0.0)
    gate = wsel / jnp.sum(wsel, axis=1, keepdims=True) * ROUTED_SCALE
    return gate, jnp.where(first, e_rank, n_exp), sel


def _out_proj_kernel(yl_ref, ya_ref, x_ref, gl_ref, ga_ref, wl_ref, wa_ref, g1_ref, sc_ref, sh_ref, g2_ref,
                     gn_ref, wrh_ref, wrl_ref, br_ref, w1_ref, w3_ref, w2_ref,
                     xs_ref, h2_ref, gate_ref, rank_ref, cum_ref, cnt_scr):
    tm = x_ref.shape[0]

    @pl.when(pl.program_id(0) == 0)
    def _():
        cnt_scr[...] = jnp.zeros_like(cnt_scr)

    yl = _rms(yl_ref[...].astype(F32), gl_ref[...]).astype(BF16)
    ya = _rms(ya_ref[...].astype(F32), ga_ref[...]).astype(BF16)
    mixed = _dot(yl, wl_ref[...]) + _dot(ya, wa_ref[...])
    x1 = x_ref[...] + g1_ref[...] * mixed
    h2 = _rms(x1, gn_ref[...]) * (1.0 + sc_ref[...]) + sh_ref[...]
    h2_ref[...] = h2

    gate, e_rank, sel = _route(h2, wrh_ref, wrl_ref, br_ref[...])
    gate_ref[...] = gate
    rank_ref[...] = e_rank
    i = lax.broadcasted_iota(jnp.int32, (tm, tm), 0)
    j = lax.broadcasted_iota(jnp.int32, (tm, tm), 1)
    lower = jnp.where(j < i, 1.0, 0.0).astype(BF16)
    selb = jnp.where(sel, 1.0, 0.0).astype(BF16)
    cum = _dot(lower, selb) + cnt_scr[...]
    cum_ref[...] = cum.astype(jnp.int32)
    cnt_scr[...] = cnt_scr[...] + jnp.sum(selb.astype(F32), axis=0, keepdims=True)

    hb = h2.astype(BF16)
    act = (_silu(_dot(hb, w1_ref[...])) * _dot(hb, w3_ref[...])).astype(BF16)
    shared = _dot(act, w2_ref[...])
    xs_ref[...] = x1 + g2_ref[...] * shared


def _out_proj_call(yl, ya, x, gl, ga, w_out, g1, sc2, sh2, g2, gn, wr_hi, wr_lo, br, w1s, w3s, w2s,
                   rows_per_mod, tm):
    t, d = x.shape
    w = yl.shape[1]
    full = lambda shape: pl.BlockSpec(shape, lambda i: (0,) * len(shape))
    mod = lambda m: _mod_specs(m, tm, d, rows_per_mod, 1)
    row = lambda n: pl.BlockSpec((tm, n), lambda i: (i, 0))
    return pl.pallas_call(
        _out_proj_kernel,
        grid=(t // tm,),
        in_specs=[row(w), row(w), row(d), full(gl.shape), full(ga.shape),
                  pl.BlockSpec((w, d), lambda i: (0, 0)), pl.BlockSpec((w, d), lambda i: (1, 0)),
                  mod(g1), mod(sc2), mod(sh2), mod(g2), full(gn.shape),
                  full(wr_hi.shape), full(wr_lo.shape), full(br.shape),
                  full(w1s.shape), full(w3s.shape), full(w2s.shape)],
        out_specs=[row(d), row(d), row(LANES), row(LANES), row(LANES)],
        out_shape=[jax.ShapeDtypeStruct((t, d), F32), jax.ShapeDtypeStruct((t, d), F32),
                   jax.ShapeDtypeStruct((t, LANES), F32), jax.ShapeDtypeStruct((t, LANES), jnp.int32),
                   jax.ShapeDtypeStruct((t, LANES), jnp.int32)],
        scratch_shapes=[pltpu.VMEM((1, LANES), F32)],
        compiler_params=_cparams(("arbitrary",)),
        name="out_proj",
    )(yl, ya, x, gl, ga, w_out, w_out, g1, sc2, sh2, g2, gn, wr_hi, wr_lo, br, w1s, w3s, w2s)


def _row_copy(src, dst, src_row, dst_row, sem):
    return pltpu.make_async_copy(src.at[pl.ds(src_row, 1), :], dst.at[pl.ds(dst_row, 1), :], sem)


def _scatter_kernel(pos_ref, h_ref, *refs):
    xs_ref, sem = refs[-2], refs[-1]
    tb = pos_ref.shape[0] // TOP_K
    t0 = pl.program_id(0) * tb

    def issue(t, c):
        for j in range(TOP_K):
            _row_copy(h_ref, xs_ref, t0 + t, pos_ref[t * TOP_K + j], sem).start()
        return c

    lax.fori_loop(0, tb, issue, 0)

    def drain(t, c):
        for j in range(TOP_K):
            _row_copy(h_ref, xs_ref, t0 + t, pos_ref[t * TOP_K + j], sem).wait()
        return c

    lax.fori_loop(0, tb, drain, 0)


def _scatter_call(h2, pos_flat, n_rows, xs_prev=None, tb=256):
    t, d = h2.shape
    any_spec = pl.BlockSpec(memory_space=pl.ANY)
    in_specs = [pl.BlockSpec((tb * TOP_K,), lambda i: (i,), memory_space=pltpu.SMEM), any_spec]
    args = [pos_flat, h2]
    aliases = {}
    if xs_prev is not None:
        in_specs.append(any_spec)
        args.append(xs_prev)
        aliases = {2: 0}
    return pl.pallas_call(
        _scatter_kernel,
        grid=(t // tb,),
        in_specs=in_specs,
        out_specs=any_spec,
        out_shape=jax.ShapeDtypeStruct((n_rows, d), F32),
        scratch_shapes=[pltpu.SemaphoreType.DMA(())],
        input_output_aliases=aliases,
        compiler_params=_cparams(("arbitrary",)),
        name="moe_scatter",
    )(*args)


def _experts_kernel(be_ref, bv_ref, bi_ref, x_ref, w1_ref, w3_ref, w2_ref, o_ref, w1_scr, w3_scr, w2_scr):
    i = pl.program_id(0)
    prev = be_ref[jnp.maximum(i - 1, 0)]

    @pl.when((i == 0) | (be_ref[i] != prev))
    def _():
        w1_scr[...] = w1_ref[...].astype(BF16)
        w3_scr[...] = w3_ref[...].astype(BF16)
        w2_scr[...] = w2_ref[...].astype(BF16)

    valid = bv_ref[i]

    @pl.when(valid > 0)
    def _():
        bm = x_ref.shape[0]
        row = lax.broadcasted_iota(jnp.int32, (bm, 1), 0)
        x = jnp.where(row < valid, x_ref[...], 0.0).astype(BF16)
        act = (_silu(_dot(x, w1_scr[...])) * _dot(x, w3_scr[...])).astype(BF16)
        o_ref[...] = _dot(act, w2_scr[...])


def _experts_call(block_e, block_valid, block_idx, xs, w1, w3, w2):
    n_rows, d = xs.shape
    n_exp, _, ff = w1.shape
    bm = MOE_BM
    return pl.pallas_call(
        _experts_kernel,
        grid_spec=pltpu.PrefetchScalarGridSpec(
            num_scalar_prefetch=3,
            grid=(n_rows // bm,),
            in_specs=[pl.BlockSpec((bm, d), lambda i, be, bv, bi: (bi[i], 0)),
                      pl.BlockSpec((None, d, ff), lambda i, be, bv, bi: (be[i], 0, 0)),
                      pl.BlockSpec((None, d, ff), lambda i, be, bv, bi: (be[i], 0, 0)),
                      pl.BlockSpec((None, ff, d), lambda i, be, bv, bi: (be[i], 0, 0))],
            out_specs=pl.BlockSpec((bm, d), lambda i, be, bv, bi: (bi[i], 0)),
            scratch_shapes=[pltpu.VMEM((d, ff), BF16), pltpu.VMEM((d, ff), BF16), pltpu.VMEM((ff, d), BF16)]),
        out_shape=jax.ShapeDtypeStruct((n_rows, d), F32),
        compiler_params=_cparams(("arbitrary",)),
        name="moe_experts",
    )(block_e, block_valid, block_idx, xs, w1, w3, w2)


def _combine_kernel(pos_ref, w_ref, xs_ref, g2_ref, gf_ref, y_hbm, o_ref, buf, sem):
    tb = xs_ref.shape[0]

    def row_copy(t, j):
        return pltpu.make_async_copy(y_hbm.at[pl.ds(pos_ref[t * TOP_K + j], 1), :],
                                     buf.at[j, pl.ds(t, 1), :], sem)

    def issue(t, c):
        for j in range(TOP_K):
            row_copy(t, j).start()
        return c

    lax.fori_loop(0, tb, issue, 0)

    def drain(t, c):
        for j in range(TOP_K):
            row_copy(t, j).wait()
        return c

    lax.fori_loop(0, tb, drain, 0)

    wts = w_ref[...]
    routed = jnp.zeros(xs_ref.shape, F32)
    for j in range(TOP_K):
        routed = routed + wts[:, j:j + 1] * buf[j]
    x2 = xs_ref[...] + g2_ref[...] * routed
    o_ref[...] = _rms(x2, gf_ref[...])


def _combine_call(pos_flat, w8, xs, g2, gf, y_sorted, rows_per_mod, tb=128):
    t, d = xs.shape
    tb = min(tb, t)
    return pl.pallas_call(
        _combine_kernel,
        grid=(t // tb,),
        in_specs=[pl.BlockSpec((tb * TOP_K,), lambda i: (i,), memory_space=pltpu.SMEM),
                  pl.BlockSpec((tb, TOP_K), lambda i: (i, 0)),
                  pl.BlockSpec((tb, d), lambda i: (i, 0)),
                  _mod_specs(g2, tb, d, rows_per_mod, 1),
                  pl.BlockSpec((1, d), lambda i: (0, 0)),
                  pl.BlockSpec(memory_space=pl.ANY)],
        out_specs=pl.BlockSpec((tb, d), lambda i: (i, 0)),
        out_shape=jax.ShapeDtypeStruct((t, d), F32),
        scratch_shapes=[pltpu.VMEM((TOP_K, tb, d), F32), pltpu.SemaphoreType.DMA(())],
        compiler_params=_cparams(("arbitrary",)),
        name="moe_combine",
    )(pos_flat, w8, xs, g2, gf, y_sorted)


def _plan(e_rank_p, cum_p, gate_p, e_rank_s, cum_s, gate_s, n_exp):
    bm = MOE_BM
    sel_p = e_rank_p[:, :n_exp] < TOP_K
    sel_s = e_rank_s[:, :n_exp] < TOP_K
    cnt_p = jnp.sum(sel_p, axis=0, dtype=jnp.int32)
    counts = cnt_p + jnp.sum(sel_s, axis=0, dtype=jnp.int32)
    padded = (counts + bm - 1) // bm * bm
    pad_end = jnp.cumsum(padded)
    pad_start = pad_end - padded
    n_assign = (e_rank_p.shape[0] + e_rank_s.shape[0]) * TOP_K
    n_blocks = -(-n_assign // bm) + n_exp
    blk_start = jnp.arange(n_blocks, dtype=jnp.int32) * bm
    block_e = jnp.minimum(jnp.searchsorted(pad_end, blk_start, side='right'), n_exp - 1).astype(jnp.int32)
    block_valid = jnp.clip(pad_start[block_e] + counts[block_e] - blk_start, 0, bm).astype(jnp.int32)
    n_used = pad_end[-1] // bm
    block_idx = jnp.minimum(jnp.arange(n_blocks, dtype=jnp.int32), jnp.maximum(n_used - 1, 0)).astype(jnp.int32)
    slots = jnp.arange(TOP_K, dtype=jnp.int32)[None, :, None]

    def compact(e_rank, cum, gate, offset):
        dest = pad_start[None, :] + offset[None, :] + cum[:, :n_exp]
        hit = e_rank[:, None, :n_exp] == slots
        pos = jnp.sum(jnp.where(hit, dest[:, None, :], 0), axis=-1, dtype=jnp.int32)
        wts = jnp.sum(jnp.where(hit, gate[:, None, :n_exp], 0.0), axis=-1)
        return pos.reshape(-1), wts

    pos_p, w_p = compact(e_rank_p, cum_p, gate_p, jnp.zeros_like(cnt_p))
    pos_s, w_s = compact(e_rank_s, cum_s, gate_s, cnt_p)
    return block_e, block_valid, block_idx, n_blocks * bm, pos_p, w_p, pos_s, w_s


def kernel(x_prompt, x_sample, c_prompt, c_sample, cache_k, cache_v, page_table, state_lru_h, state_conv, w_mod, b_mod, g_norm1, g_norm2, w_in, conv_w, conv_b, w_gate_a, b_gate_a, w_gate_x, b_gate_x, lru_lambda, sb_bias, g_out_lru, g_out_att, w_out, w_router, b_router, w1_e, w3_e, w2_e, w1_s, w3_s, w2_s, g_final):
    nb, s, d = x_prompt.shape
    db = x_sample.shape[0]
    w = conv_w.shape[-1]
    n_exp = w_router.shape[-1]
    l = 0
    row = lambda v: v.reshape(1, -1)

    m = _mod_call(jnp.concatenate([c_prompt, c_sample], axis=0), w_mod[l], row(b_mod[l]))
    mods_p = [m[:nb, i * d:(i + 1) * d].reshape(nb, 1, d) for i in range(6)]
    mods_s = [m[nb:, i * d:(i + 1) * d] for i in range(6)]

    w_in_b = w_in[l].astype(BF16)
    w_out_b = w_out[l].astype(BF16)
    wa_b = w_gate_a[l].astype(BF16)
    wx_b = w_gate_x[l].astype(BF16)
    wr2 = jnp.concatenate([w_router[l], w_router[l]], axis=1)
    wr_hi = wr2.astype(BF16)
    wr_lo = (wr2 - wr_hi.astype(F32)).astype(BF16)
    br2 = row(jnp.concatenate([b_router[l], b_router[l]]).astype(F32))
    w1s_b, w3s_b, w2s_b = w1_s[l].astype(BF16), w3_s[l].astype(BF16), w2_s[l].astype(BF16)
    lru_args = (conv_w[l], row(conv_b[l]), wa_b, row(b_gate_a[l]), wx_b, row(b_gate_x[l]), row(lru_lambda[l]))

    def mix_and_route(yl, ya, x, mods, rows_per_mod, tm):
        return _out_proj_call(yl, ya, x, row(g_out_lru[l]), row(g_out_att[l]), w_out_b, mods[2], mods[4], mods[3],
                              mods[5], row(g_norm2[l]), wr_hi, wr_lo, br2, w1s_b, w3s_b, w2s_b, rows_per_mod, tm)

    xp = x_prompt.reshape(nb * s, d)
    proj_p = _in_proj_call(xp, row(g_norm1[l]), mods_p[1], mods_p[0], w_in_b, s, 512)
    yl_p, hl_p = _lru_prompt_call(proj_p, nb, *lru_args)
    ya_p = _attn_prompt_call(proj_p, nb, sb_bias[l])
    xs_p, h2_p, gate_p, rank_p, cum_p = mix_and_route(yl_p.reshape(nb * s, w), ya_p, xp, mods_p, s, 256)

    xsm = x_sample.reshape(db, d)
    proj_s = _in_proj_call(xsm, row(g_norm1[l]), mods_s[1], mods_s[0], w_in_b, 1, db)
    yl_s, hl_s = _lru_sample_call(proj_s, state_conv[l].reshape(db, (CONV_W - 1) * w), state_lru_h[l], *lru_args)
    ya_s = _attn_sample_call(proj_s[2], cache_k[l], cache_v[l], page_table, sb_bias[l])
    xs_s, h2_s, gate_s, rank_s, cum_s = mix_and_route(yl_s, ya_s, xsm, mods_s, 1, db)

    block_e, block_valid, block_idx, n_rows, pos_p, w8_p, pos_s, w8_s = _plan(
        rank_p, cum_p, gate_p, rank_s, cum_s, gate_s, n_exp)
    x_sorted = _scatter_call(h2_p, pos_p, n_rows)
    x_sorted = _scatter_call(h2_s, pos_s, n_rows, xs_prev=x_sorted, tb=db)
    y_sorted = _experts_call(block_e, block_valid, block_idx, x_sorted, w1_e[l], w3_e[l], w2_e[l])
    y_p = _combine_call(pos_p, w8_p, xs_p, mods_p[5], row(g_final), y_sorted, s)
    y_s = _combine_call(pos_s, w8_s, xs_s, mods_s[5], row(g_final), y_sorted, 1)

    nh = w // HEAD_DIM
    tail = CONV_W - 1
    k_p = proj_p[3].reshape(1, nb, s, nh, HEAD_DIM)
    v_p = proj_p[4].reshape(1, nb, s, nh, HEAD_DIM)
    conv_p = proj_p[0].reshape(nb, s, w)[:, s - tail:, :][None]
    k_s = proj_s[3].reshape(1, db, 1, nh, HEAD_DIM)
    v_s = proj_s[4].reshape(1, db, 1, nh, HEAD_DIM)
    conv_s = jnp.concatenate([state_conv[l][:, 1:, :], proj_s[0][:, None, :]], axis=1)[None]
    return (y_p.reshape(nb, s, d), y_s.reshape(db, 1, d), k_p, v_p, hl_p[None], conv_p,
            k_s, v_s, hl_s[None], conv_s)
```

```python
import functools

import jax
import jax.numpy as jnp
from jax import lax
from jax.experimental import pallas as pl
from jax.experimental.pallas import tpu as pltpu

F32 = jnp.float32
BF16 = jnp.bfloat16

EPS = 1e-6
LRU_C = 8.0
CONV_W = 4
LRU_BLOCKS = 8
N_GROUPS = 8
TOPK_GROUPS = 4
TOP_K = 8
ROUTED_SCALE = 2.5
LANES = 128
HEAD_DIM = 128
MOE_BM = 256
VMEM_LIMIT = 56 * 1024 * 1024


def _cparams(sem):
    return pltpu.CompilerParams(dimension_semantics=sem, vmem_limit_bytes=VMEM_LIMIT)


def _dot(a, b):
    return jnp.dot(a, b, preferred_element_type=F32)


def _dot_nt(a, b):
    return lax.dot_general(a, b, (((1,), (1,)), ((), ())), preferred_element_type=F32)


def _split(x):
    hi = x.astype(BF16)
    lo = (x - hi.astype(F32)).astype(BF16)
    return hi, lo


def _softplus(z):
    return jnp.maximum(z, 0.0) + jnp.log(1.0 + jnp.exp(-jnp.abs(z)))


def _sigmoid(z):
    return 1.0 / (1.0 + jnp.exp(-z))


def _silu(z):
    return z * _sigmoid(z)


def _gelu_tanh(x):
    return 0.5 * x * (1.0 + jnp.tanh(0.7978845608028654 * (x + 0.044715 * (x * x * x))))


def _rms(x, g):
    return x * lax.rsqrt(jnp.mean(x * x, axis=-1, keepdims=True) + EPS) * g


def _mod_kernel(c_ref, w_ref, b_ref, o_ref):
    a = _silu(c_ref[...]).astype(BF16)
    o_ref[...] = _dot(a, w_ref[...].astype(BF16)) + b_ref[...]


def _mod_call(c_all, w_mod, b_mod, tn=512):
    m, d = c_all.shape
    n = w_mod.shape[1]
    return pl.pallas_call(
        _mod_kernel,
        grid=(n // tn,),
        in_specs=[pl.BlockSpec((m, d), lambda j: (0, 0)),
                  pl.BlockSpec((d, tn), lambda j: (0, j)),
                  pl.BlockSpec((1, tn), lambda j: (0, j))],
        out_specs=pl.BlockSpec((m, tn), lambda j: (0, j)),
        out_shape=jax.ShapeDtypeStruct((m, n), F32),
        compiler_params=_cparams(("arbitrary",)),
        name="mod",
    )(c_all, w_mod, b_mod)


def _mod_specs(mod, tm, d, rows_per_mod, n_grid_axes):
    if mod.ndim == 3:
        tiles = rows_per_mod // tm
        if n_grid_axes == 1:
            return pl.BlockSpec((None, 1, d), lambda i: (i // tiles, 0, 0))
        return pl.BlockSpec((None, 1, d), lambda i, j: (i // tiles, 0, 0))
    if n_grid_axes == 1:
        return pl.BlockSpec((tm, d), lambda i: (i, 0))
    return pl.BlockSpec((tm, d), lambda i, j: (i, 0))


def _in_proj_kernel(x_ref, g_ref, sc_ref, sh_ref, w_ref, o_ref, h_scr):
    @pl.when(pl.program_id(1) == 0)
    def _():
        h = _rms(x_ref[...], g_ref[...]) * (1.0 + sc_ref[...]) + sh_ref[...]
        h_scr[...] = h.astype(BF16)

    o_ref[...] = _dot(h_scr[...], w_ref[...])


def _in_proj_call(x, g, sc, sh, w_in, rows_per_mod, tm):
    t, d = x.shape
    w = w_in.shape[1] // 5
    return pl.pallas_call(
        _in_proj_kernel,
        grid=(t // tm, 5),
        in_specs=[pl.BlockSpec((tm, d), lambda i, j: (i, 0)),
                  pl.BlockSpec((1, d), lambda i, j: (0, 0)),
                  _mod_specs(sc, tm, d, rows_per_mod, 2),
                  _mod_specs(sh, tm, d, rows_per_mod, 2),
                  pl.BlockSpec((d, w), lambda i, j: (0, j))],
        out_specs=pl.BlockSpec((None, tm, w), lambda i, j: (j, i, 0)),
        out_shape=jax.ShapeDtypeStruct((5, t, w), F32),
        scratch_shapes=[pltpu.VMEM((tm, d), BF16)],
        compiler_params=_cparams(("arbitrary", "arbitrary")),
        name="in_proj",
    )(x, g, sc, sh, w_in)


def _lru_gates(xc, wa_ref, ba, wx_ref, bx, sp):
    xb = xc.astype(BF16)
    blk = xc.shape[1] // LRU_BLOCKS
    za = jnp.concatenate([_dot(xb[:, n * blk:(n + 1) * blk], wa_ref[n]) for n in range(LRU_BLOCKS)], axis=1)
    zx = jnp.concatenate([_dot(xb[:, n * blk:(n + 1) * blk], wx_ref[n]) for n in range(LRU_BLOCKS)], axis=1)
    r = _sigmoid(za + ba)
    i = _sigmoid(zx + bx)
    log_a = -LRU_C * r * sp
    a = jnp.exp(log_a)
    u = jnp.sqrt(1.0 - jnp.exp(2.0 * log_a)) * (i * xc)
    return a, u


def _lru_prompt_kernel(xl_ref, gt_ref, cw_ref, cb_ref, wa_ref, ba_ref, wx_ref, bx_ref, lam_ref,
                       y_ref, hl_ref, xp_scr, h_scr, a_scr, u_scr):
    nb, tc, w = xl_ref.shape
    pad = xp_scr.shape[1] - tc
    nc = w // LANES

    @pl.when(pl.program_id(0) == 0)
    def _():
        xp_scr[:, 0:pad, :] = jnp.zeros((nb, pad, w), F32)
        h_scr[...] = jnp.zeros_like(h_scr)

    sp = _softplus(-lam_ref[...])
    cw = cw_ref[...]
    for b in range(nb):
        x = xl_ref[b]
        xp_scr[b, pad:pad + tc, :] = x
        xc = cw[CONV_W - 1:CONV_W, :] * x + cb_ref[...]
        for j in range(CONV_W - 1):
            off = pad - (CONV_W - 1) + j
            xc = xc + cw[j:j + 1, :] * xp_scr[b, off:off + tc, :]
        a, u = _lru_gates(xc, wa_ref, ba_ref[...], wx_ref, bx_ref[...], sp)
        for c in range(nc):
            a_scr[c, b * tc:(b + 1) * tc, :] = a[:, c * LANES:(c + 1) * LANES]
            u_scr[c, b * tc:(b + 1) * tc, :] = u[:, c * LANES:(c + 1) * LANES]
        xp_scr[b, 0:pad, :] = x[tc - pad:tc, :]

    h = [h_scr[:, c * LANES:(c + 1) * LANES] for c in range(nc)]
    for t in range(tc):
        rows = pl.ds(t, nb, stride=tc)
        for c in range(nc):
            h[c] = a_scr[c, rows, :] * h[c] + u_scr[c, rows, :]
            u_scr[c, rows, :] = h[c]
    h = jnp.concatenate(h, axis=1)
    h_scr[...] = h
    hl_ref[...] = h

    for b in range(nb):
        hs = jnp.concatenate([u_scr[c, b * tc:(b + 1) * tc, :] for c in range(nc)], axis=1)
        y_ref[b] = (_gelu_tanh(gt_ref[b]) * hs).astype(BF16)


def _lru_prompt_call(proj5, nb, cw, cb, wa, ba, wx, bx, lam, tc=128):
    _, t, w = proj5.shape
    s = t // nb
    p4 = proj5.reshape(5, nb, s, w)
    full = lambda shape: pl.BlockSpec(shape, lambda c: (0,) * len(shape))
    return pl.pallas_call(
        _lru_prompt_kernel,
        grid=(s // tc,),
        in_specs=[pl.BlockSpec((None, nb, tc, w), lambda c: (0, 0, c, 0)),
                  pl.BlockSpec((None, nb, tc, w), lambda c: (1, 0, c, 0)),
                  full(cw.shape), full(cb.shape), full(wa.shape), full(ba.shape),
                  full(wx.shape), full(bx.shape), full(lam.shape)],
        out_specs=[pl.BlockSpec((nb, tc, w), lambda c: (0, c, 0)),
                   pl.BlockSpec((nb, w), lambda c: (0, 0))],
        out_shape=[jax.ShapeDtypeStruct((nb, s, w), BF16),
                   jax.ShapeDtypeStruct((nb, w), F32)],
        scratch_shapes=[pltpu.VMEM((nb, tc + 8, w), F32),
                        pltpu.VMEM((nb, w), F32),
                        pltpu.VMEM((w // LANES, nb * tc, LANES), F32),
                        pltpu.VMEM((w // LANES, nb * tc, LANES), F32)],
        compiler_params=_cparams(("arbitrary",)),
        name="lru_prompt",
    )(p4, p4, cw, cb, wa, ba, wx, bx, lam)


def _lru_sample_kernel(xl_ref, gt_ref, cs_ref, h0_ref, cw_ref, cb_ref, wa_ref, ba_ref, wx_ref, bx_ref,
                       lam_ref, y_ref, hl_ref):
    w = xl_ref.shape[1]
    x = xl_ref[...]
    cw = cw_ref[...]
    xc = cw[CONV_W - 1:CONV_W, :] * x + cb_ref[...]
    for j in range(CONV_W - 1):
        xc = xc + cw[j:j + 1, :] * cs_ref[:, j * w:(j + 1) * w]
    a, u = _lru_gates(xc, wa_ref, ba_ref[...], wx_ref, bx_ref[...], _softplus(-lam_ref[...]))
    h = a * h0_ref[...] + u
    hl_ref[...] = h
    y_ref[...] = (_gelu_tanh(gt_ref[...]) * h).astype(BF16)


def _lru_sample_call(proj5, conv_state, h0, cw, cb, wa, ba, wx, bx, lam):
    _, t, w = proj5.shape
    full = lambda shape: pl.BlockSpec(shape, lambda c: (0,) * len(shape))
    return pl.pallas_call(
        _lru_sample_kernel,
        grid=(1,),
        in_specs=[pl.BlockSpec((None, t, w), lambda c: (0, 0, 0)),
                  pl.BlockSpec((None, t, w), lambda c: (1, 0, 0)),
                  full(conv_state.shape), full(h0.shape),
                  full(cw.shape), full(cb.shape), full(wa.shape), full(ba.shape),
                  full(wx.shape), full(bx.shape), full(lam.shape)],
        out_specs=[full((t, w)), full((t, w))],
        out_shape=[jax.ShapeDtypeStruct((t, w), BF16), jax.ShapeDtypeStruct((t, w), F32)],
        compiler_params=_cparams(("arbitrary",)),
        name="lru_sample",
    )(proj5, proj5, conv_state, h0, cw, cb, wa, ba, wx, bx, lam)


def _strict_upper(n):
    j = lax.broadcasted_iota(jnp.int32, (n, n), 0)
    s = lax.broadcasted_iota(jnp.int32, (n, n), 1)
    return jnp.where(j > s, 1.0, 0.0).astype(BF16)


def _sb_block(q, k, v, bias, upper, r_run, acc, scale, causal):
    z = _dot_nt(q, k) * scale + bias
    sp = _softplus(z)
    lk = -sp
    if causal is not None:
        lk = jnp.where(causal, lk, 0.0)
    hi, lo = _split(lk)
    later = _dot(hi, upper) + _dot(lo, upper)
    p = jnp.exp((z - sp) + later + r_run)
    if causal is not None:
        p = jnp.where(causal, p, 0.0)
    acc = acc + _dot(p.astype(BF16), v)
    r_run = r_run + jnp.sum(lk, axis=1, keepdims=True)
    return r_run, acc


def _attn_prompt_kernel(q_ref, k_ref, v_ref, b_ref, o_ref):
    tq = q_ref.shape[0]
    hd = HEAD_DIM
    heads = q_ref.shape[1] // hd
    qi = pl.program_id(2)
    scale = hd ** -0.5
    upper = _strict_upper(tq)
    row = lax.broadcasted_iota(jnp.int32, (tq, tq), 0)
    col = lax.broadcasted_iota(jnp.int32, (tq, tq), 1)
    qs = [q_ref[:, h * hd:(h + 1) * hd].astype(BF16) for h in range(heads)]
    biases = [b_ref[h] for h in range(heads)]

    def blocks(kb, carry, causal):
        start = pl.multiple_of(kb * tq, tq)
        out = []
        for h in range(heads):
            k = k_ref[pl.ds(start, tq), h * hd:(h + 1) * hd].astype(BF16)
            v = v_ref[pl.ds(start, tq), h * hd:(h + 1) * hd].astype(BF16)
            out.append(_sb_block(qs[h], k, v, biases[h], upper, carry[h][0], carry[h][1], scale, causal))
        return tuple(out)

    init = tuple((jnp.zeros((tq, 1), F32), jnp.zeros((tq, hd), F32)) for _ in range(heads))
    carry = blocks(qi, init, col < row)
    carry = lax.fori_loop(0, qi, lambda n, c: blocks(qi - 1 - n, c, None), carry)
    for h in range(heads):
        o_ref[:, h * hd:(h + 1) * hd] = carry[h][1].astype(BF16)


def _attn_prompt_call(proj5, nb, sb_bias, tq=256, heads=2):
    _, t, w = proj5.shape
    s = t // nb
    nh = w // HEAD_DIM
    nq = s // tq
    hw = heads * HEAD_DIM
    bias = jnp.broadcast_to(sb_bias.astype(F32)[:, None, None], (nh, 1, tq))
    return pl.pallas_call(
        _attn_prompt_kernel,
        grid=(nb, nh // heads, nq),
        in_specs=[pl.BlockSpec((None, tq, hw), lambda b, h, i: (2, b * nq + i, h)),
                  pl.BlockSpec((None, s, hw), lambda b, h, i: (3, b, h)),
                  pl.BlockSpec((None, s, hw), lambda b, h, i: (4, b, h)),
                  pl.BlockSpec((heads, 1, tq), lambda b, h, i: (h, 0, 0))],
        out_specs=pl.BlockSpec((tq, hw), lambda b, h, i: (b * nq + i, h)),
        out_shape=jax.ShapeDtypeStruct((t, w), BF16),
        compiler_params=_cparams(("arbitrary", "arbitrary", "arbitrary")),
        name="attn_prompt",
    )(proj5, proj5, proj5, bias)


def _attn_sample_kernel(pt_ref, q_ref, b_ref, *refs, pages_per_step):
    k_refs = refs[:pages_per_step]
    v_refs = refs[pages_per_step:2 * pages_per_step]
    o_ref, r_scr, acc_scr = refs[2 * pages_per_step:]
    nh, hd = q_ref.shape
    page = k_refs[0].shape[0] // nh
    step = pl.program_id(1)

    @pl.when(step == 0)
    def _():
        r_scr[...] = jnp.zeros_like(r_scr)
        acc_scr[...] = jnp.zeros_like(acc_scr)

    head = lax.broadcasted_iota(jnp.int32, (nh, hd), 0)
    q = q_ref[...].astype(BF16)
    n_keys = pages_per_step * page
    shift = page.bit_length() - 1
    j = lax.broadcasted_iota(jnp.int32, (n_keys, n_keys), 0)
    s = lax.broadcasted_iota(jnp.int32, (n_keys, n_keys), 1)
    pj, ps = j >> shift, s >> shift
    upper = jnp.where((pj < ps) | ((pj == ps) & (j > s)), 1.0, 0.0).astype(BF16)
    zs = []
    for g in range(pages_per_step):
        z = jnp.zeros((nh, page), F32)
        for h in range(nh):
            kh = k_refs[g][pl.ds(h, page, stride=nh), :].astype(BF16)
            z = jnp.where(head == h, _dot_nt(q, kh), z)
        zs.append(z)
    z = jnp.concatenate(zs, axis=1) * (hd ** -0.5) + b_ref[...]
    sp = _softplus(z)
    lk = -sp
    hi, lo = _split(lk)
    later = _dot(hi, upper) + _dot(lo, upper)
    p = jnp.exp((z - sp) + later + r_scr[...]).astype(BF16)
    acc = acc_scr[...]
    for g in range(pages_per_step):
        pg = p[:, g * page:(g + 1) * page]
        for h in range(nh):
            vh = v_refs[g][pl.ds(h, page, stride=nh), :].astype(BF16)
            acc = acc + jnp.where(head == h, _dot(pg, vh), 0.0)
    r_scr[...] = r_scr[...] + jnp.sum(lk, axis=1, keepdims=True)
    acc_scr[...] = acc

    @pl.when(step == pl.num_programs(1) - 1)
    def _():
        o_ref[...] = acc.astype(BF16)


def _attn_sample_call(q, cache_k, cache_v, page_table, sb_bias, pages_per_step=4):
    db, w = q.shape
    n_pool, page, nh, hd = cache_k.shape
    n_pages = page_table.shape[1]
    ck = cache_k.reshape(n_pool, page * nh, hd)
    cv = cache_v.reshape(n_pool, page * nh, hd)
    bias = jnp.broadcast_to(sb_bias.astype(F32)[:, None], (nh, pages_per_step * page))

    def page_spec(g):
        return pl.BlockSpec((None, page * nh, hd),
                            lambda b, s, pt: (pt[b, n_pages - 1 - (s * pages_per_step + g)], 0, 0))

    specs = [page_spec(g) for g in range(pages_per_step)]
    return pl.pallas_call(
        functools.partial(_attn_sample_kernel, pages_per_step=pages_per_step),
        grid_spec=pltpu.PrefetchScalarGridSpec(
            num_scalar_prefetch=1,
            grid=(db, n_pages // pages_per_step),
            in_specs=[pl.BlockSpec((None, nh, hd), lambda b, s, pt: (b, 0, 0)),
                      pl.BlockSpec((nh, pages_per_step * page), lambda b, s, pt: (0, 0))] + specs + specs,
            out_specs=pl.BlockSpec((None, nh, hd), lambda b, s, pt: (b, 0, 0)),
            scratch_shapes=[pltpu.VMEM((nh, 1), F32), pltpu.VMEM((nh, hd), F32)]),
        out_shape=jax.ShapeDtypeStruct((db, nh, hd), BF16),
        compiler_params=_cparams(("arbitrary", "arbitrary")),
        name="attn_sample",
    )(page_table, q.reshape(db, nh, hd), bias, *([ck] * pages_per_step), *([cv] * pages_per_step)).reshape(db, w)


def _xor_partner(x, lane, s):
    n = x.shape[1]
    up = pltpu.roll(x, n - s, axis=1)
    dn = pltpu.roll(x, s, axis=1)
    return jnp.where((lane & s) == 0, up, dn)


def _route(h2, wr_hi_ref, wr_lo_ref, br):
    tm = h2.shape[0]
    n_exp = LANES // 2
    per_group = n_exp // N_GROUPS
    h_hi, h_lo = _split(h2)
    logits = _dot(h_hi, wr_hi_ref[...]) + _dot(h_lo, wr_hi_ref[...]) + _dot(h_hi, wr_lo_ref[...])
    scores = _sigmoid(logits)
    biased = scores + br
    lane = lax.broadcasted_iota(jnp.int32, (tm, LANES), 1)
    e = lane & (n_exp - 1)
    g = e >> (per_group.bit_length() - 1)
    m1 = biased
    m2 = jnp.full_like(biased, -jnp.inf)
    for s in (1, 2, 4):
        p1 = _xor_partner(m1, lane, s)
        p2 = _xor_partner(m2, lane, s)
        m1, m2 = jnp.maximum(m1, p1), jnp.maximum(jnp.minimum(m1, p1), jnp.maximum(m2, p2))
    gs = m1 + m2
    g_rank = jnp.zeros((tm, LANES), jnp.int32)
    for k in range(1, N_GROUPS):
        other = pltpu.roll(gs, per_group * k, axis=1)
        beats = (other > gs) | ((other == gs) & (g >= k))
        g_rank = g_rank + beats.astype(jnp.int32)
    masked = jnp.where(g_rank < TOPK_GROUPS, biased, -jnp.inf)
    e_rank = jnp.zeros((tm, LANES), jnp.int32)
    for k in range(1, n_exp):
        other = pltpu.roll(masked, k, axis=1)
        beats = (other > masked) | ((other == masked) & (e >= k))
        e_rank = e_rank + beats.astype(jnp.int32)
    first = lane < n_exp
    sel = (e_rank < TOP_K) & first
    wsel = jnp.where(sel, scores, 0.0)
    gate = wsel / jnp.sum(wsel, axis=1, keepdims=True) * ROUTED_SCALE
    return gate, jnp.where(first, e_rank, n_exp), sel


def _out_proj_kernel(yl_ref, ya_ref, x_ref, gl_ref, ga_ref, wl_ref, wa_ref, g1_ref, sc_ref, sh_ref, g2_ref,
                     gn_ref, wrh_ref, wrl_ref, br_ref, w1_ref, w3_ref, w2_ref,
                     xs_ref, h2_ref, gate_ref, rank_ref, cum_ref, cnt_scr):
    tm = x_ref.shape[0]

    @pl.when(pl.program_id(0) == 0)
    def _():
        cnt_scr[...] = jnp.zeros_like(cnt_scr)

    yl = _rms(yl_ref[...].astype(F32), gl_ref[...]).astype(BF16)
    ya = _rms(ya_ref[...].astype(F32), ga_ref[...]).astype(BF16)
    mixed = _dot(yl, wl_ref[...]) + _dot(ya, wa_ref[...])
    x1 = x_ref[...] + g1_ref[...] * mixed
    h2 = _rms(x1, gn_ref[...]) * (1.0 + sc_ref[...]) + sh_ref[...]
    h2_ref[...] = h2

    gate, e_rank, sel = _route(h2, wrh_ref, wrl_ref, br_ref[...])
    gate_ref[...] = gate
    rank_ref[...] = e_rank
    i = lax.broadcasted_iota(jnp.int32, (tm, tm), 0)
    j = lax.broadcasted_iota(jnp.int32, (tm, tm), 1)
    lower = jnp.where(j < i, 1.0, 0.0).astype(BF16)
    selb = jnp.where(sel, 1.0, 0.0).astype(BF16)
    cum = _dot(lower, selb) + cnt_scr[...]
    cum_ref[...] = cum.astype(jnp.int32)
    cnt_scr[...] = cnt_scr[...] + jnp.sum(selb.astype(F32), axis=0, keepdims=True)

    hb = h2.astype(BF16)
    act = (_silu(_dot(hb, w1_ref[...])) * _dot(hb, w3_ref[...])).astype(BF16)
    shared = _dot(act, w2_ref[...])
    xs_ref[...] = x1 + g2_ref[...] * shared


def _out_proj_call(yl, ya, x, gl, ga, w_out, g1, sc2, sh2, g2, gn, wr_hi, wr_lo, br, w1s, w3s, w2s,
                   rows_per_mod, tm):
    t, d = x.shape
    w = yl.shape[1]
    full = lambda shape: pl.BlockSpec(shape, lambda i: (0,) * len(shape))
    mod = lambda m: _mod_specs(m, tm, d, rows_per_mod, 1)
    row = lambda n: pl.BlockSpec((tm, n), lambda i: (i, 0))
    return pl.pallas_call(
        _out_proj_kernel,
        grid=(t // tm,),
        in_specs=[row(w), row(w), row(d), full(gl.shape), full(ga.shape),
                  pl.BlockSpec((w, d), lambda i: (0, 0)), pl.BlockSpec((w, d), lambda i: (1, 0)),
                  mod(g1), mod(sc2), mod(sh2), mod(g2), full(gn.shape),
                  full(wr_hi.shape), full(wr_lo.shape), full(br.shape),
                  full(w1s.shape), full(w3s.shape), full(w2s.shape)],
        out_specs=[row(d), row(d), row(LANES), row(LANES), row(LANES)],
        out_shape=[jax.ShapeDtypeStruct((t, d), F32), jax.ShapeDtypeStruct((t, d), F32),
                   jax.ShapeDtypeStruct((t, LANES), F32), jax.ShapeDtypeStruct((t, LANES), jnp.int32),
                   jax.ShapeDtypeStruct((t, LANES), jnp.int32)],
        scratch_shapes=[pltpu.VMEM((1, LANES), F32)],
        compiler_params=_cparams(("arbitrary",)),
        name="out_proj",
    )(yl, ya, x, gl, ga, w_out, w_out, g1, sc2, sh2, g2, gn, wr_hi, wr_lo, br, w1s, w3s, w2s)


def _row_copy(src, dst, src_row, dst_row, sem):
    return pltpu.make_async_copy(src.at[pl.ds(src_row, 1), :], dst.at[pl.ds(dst_row, 1), :], sem)


def _scatter_kernel(pos_ref, h_ref, *refs):
    xs_ref, sem = refs[-2], refs[-1]
    tb = h_ref.shape[0]

    def issue(t, c):
        for j in range(TOP_K):
            _row_copy(h_ref, xs_ref, t, pos_ref[t * TOP_K + j], sem).start()
        return c

    lax.fori_loop(0, tb, issue, 0)

    def drain(t, c):
        for j in range(TOP_K):
            _row_copy(h_ref, xs_ref, t, pos_ref[t * TOP_K + j], sem).wait()
        return c

    lax.fori_loop(0, tb, drain, 0)


def _scatter_call(h2, pos_flat, n_rows, xs_prev=None, tb=256):
    t, d = h2.shape
    any_spec = pl.BlockSpec(memory_space=pl.ANY)
    in_specs = [pl.BlockSpec((tb * TOP_K,), lambda i: (i,), memory_space=pltpu.SMEM),
                pl.BlockSpec((tb, d), lambda i: (i, 0))]
    args = [pos_flat, h2]
    aliases = {}
    if xs_prev is not None:
        in_specs.append(any_spec)
        args.append(xs_prev)
        aliases = {2: 0}
    return pl.pallas_call(
        _scatter_kernel,
        grid=(t // tb,),
        in_specs=in_specs,
        out_specs=any_spec,
        out_shape=jax.ShapeDtypeStruct((n_rows, d), F32),
        scratch_shapes=[pltpu.SemaphoreType.DMA(())],
        input_output_aliases=aliases,
        compiler_params=_cparams(("arbitrary",)),
        name="moe_scatter",
    )(*args)


def _experts_kernel(be_ref, bv_ref, bi_ref, x_ref, w1_ref, w3_ref, w2_ref, o_ref, w1_scr, w3_scr, w2_scr):
    i = pl.program_id(0)
    prev = be_ref[jnp.maximum(i - 1, 0)]

    @pl.when((i == 0) | (be_ref[i] != prev))
    def _():
        w1_scr[...] = w1_ref[...].astype(BF16)
        w3_scr[...] = w3_ref[...].astype(BF16)
        w2_scr[...] = w2_ref[...].astype(BF16)

    valid = bv_ref[i]

    @pl.when(valid > 0)
    def _():
        bm = x_ref.shape[0]
        row = lax.broadcasted_iota(jnp.int32, (bm, 1), 0)
        x = jnp.where(row < valid, x_ref[...], 0.0).astype(BF16)
        act = (_silu(_dot(x, w1_scr[...])) * _dot(x, w3_scr[...])).astype(BF16)
        o_ref[...] = _dot(act, w2_scr[...])


def _experts_call(block_e, block_valid, block_idx, xs, w1, w3, w2):
    n_rows, d = xs.shape
    n_exp, _, ff = w1.shape
    bm = MOE_BM
    return pl.pallas_call(
        _experts_kernel,
        grid_spec=pltpu.PrefetchScalarGridSpec(
            num_scalar_prefetch=3,
            grid=(n_rows // bm,),
            in_specs=[pl.BlockSpec((bm, d), lambda i, be, bv, bi: (bi[i], 0)),
                      pl.BlockSpec((None, d, ff), lambda i, be, bv, bi: (be[i], 0, 0)),
                      pl.BlockSpec((None, d, ff), lambda i, be, bv, bi: (be[i], 0, 0)),
                      pl.BlockSpec((None, ff, d), lambda i, be, bv, bi: (be[i], 0, 0))],
            out_specs=pl.BlockSpec((bm, d), lambda i, be, bv, bi: (bi[i], 0)),
            scratch_shapes=[pltpu.VMEM((d, ff), BF16), pltpu.VMEM((d, ff), BF16), pltpu.VMEM((ff, d), BF16)]),
        out_shape=jax.ShapeDtypeStruct((n_rows, d), F32),
        compiler_params=_cparams(("arbitrary",)),
        name="moe_experts",
    )(block_e, block_valid, block_idx, xs, w1, w3, w2)


def _combine_kernel(pos_ref, w_ref, xs_ref, g2_ref, gf_ref, y_hbm, o_ref, buf, sem):
    tb = xs_ref.shape[0]

    def row_copy(t, j):
        return pltpu.make_async_copy(y_hbm.at[pl.ds(pos_ref[t * TOP_K + j], 1), :],
                                     buf.at[j, pl.ds(t, 1), :], sem)

    def issue(t, c):
        for j in range(TOP_K):
            row_copy(t, j).start()
        return c

    lax.fori_loop(0, tb, issue, 0)

    def drain(t, c):
        for j in range(TOP_K):
            row_copy(t, j).wait()
        return c

    lax.fori_loop(0, tb, drain, 0)

    wts = w_ref[...]
    routed = jnp.zeros(xs_ref.shape, F32)
    for j in range(TOP_K):
        routed = routed + wts[:, j:j + 1] * buf[j]
    x2 = xs_ref[...] + g2_ref[...] * routed
    o_ref[...] = _rms(x2, gf_ref[...])


def _combine_call(pos_flat, w8, xs, g2, gf, y_sorted, rows_per_mod, tb=128):
    t, d = xs.shape
    tb = min(tb, t)
    return pl.pallas_call(
        _combine_kernel,
        grid=(t // tb,),
        in_specs=[pl.BlockSpec((tb * TOP_K,), lambda i: (i,), memory_space=pltpu.SMEM),
                  pl.BlockSpec((tb, TOP_K), lambda i: (i, 0)),
                  pl.BlockSpec((tb, d), lambda i: (i, 0)),
                  _mod_specs(g2, tb, d, rows_per_mod, 1),
                  pl.BlockSpec((1, d), lambda i: (0, 0)),
                  pl.BlockSpec(memory_space=pl.ANY)],
        out_specs=pl.BlockSpec((tb, d), lambda i: (i, 0)),
        out_shape=jax.ShapeDtypeStruct((t, d), F32),
        scratch_shapes=[pltpu.VMEM((TOP_K, tb, d), F32), pltpu.SemaphoreType.DMA(())],
        compiler_params=_cparams(("arbitrary",)),
        name="moe_combine",
    )(pos_flat, w8, xs, g2, gf, y_sorted)


def _plan(e_rank_p, cum_p, gate_p, e_rank_s, cum_s, gate_s, n_exp):
    bm = MOE_BM
    sel_p = e_rank_p[:, :n_exp] < TOP_K
    sel_s = e_rank_s[:, :n_exp] < TOP_K
    cnt_p = jnp.sum(sel_p, axis=0, dtype=jnp.int32)
    counts = cnt_p + jnp.sum(sel_s, axis=0, dtype=jnp.int32)
    padded = (counts + bm - 1) // bm * bm
    pad_end = jnp.cumsum(padded)
    pad_start = pad_end - padded
    n_assign = (e_rank_p.shape[0] + e_rank_s.shape[0]) * TOP_K
    n_blocks = -(-n_assign // bm) + n_exp
    blk_start = jnp.arange(n_blocks, dtype=jnp.int32) * bm
    block_e = jnp.minimum(jnp.sum(pad_end[None, :] <= blk_start[:, None], axis=1, dtype=jnp.int32), n_exp - 1)
    block_valid = jnp.clip(pad_start[block_e] + counts[block_e] - blk_start, 0, bm).astype(jnp.int32)
    n_used = pad_end[-1] // bm
    block_idx = jnp.minimum(jnp.arange(n_blocks, dtype=jnp.int32), jnp.maximum(n_used - 1, 0)).astype(jnp.int32)
    slots = jnp.arange(TOP_K, dtype=jnp.int32)[None, :, None]

    def compact(e_rank, cum, gate, offset):
        dest = pad_start[None, :] + offset[None, :] + cum[:, :n_exp]
        hit = e_rank[:, None, :n_exp] == slots
        pos = jnp.sum(jnp.where(hit, dest[:, None, :], 0), axis=-1, dtype=jnp.int32)
        wts = jnp.sum(jnp.where(hit, gate[:, None, :n_exp], 0.0), axis=-1)
        return pos.reshape(-1), wts

    pos_p, w_p = compact(e_rank_p, cum_p, gate_p, jnp.zeros_like(cnt_p))
    pos_s, w_s = compact(e_rank_s, cum_s, gate_s, cnt_p)
    return block_e, block_valid, block_idx, n_blocks * bm, pos_p, w_p, pos_s, w_s


def kernel(x_prompt, x_sample, c_prompt, c_sample, cache_k, cache_v, page_table, state_lru_h, state_conv, w_mod, b_mod, g_norm1, g_norm2, w_in, conv_w, conv_b, w_gate_a, b_gate_a, w_gate_x, b_gate_x, lru_lambda, sb_bias, g_out_lru, g_out_att, w_out, w_router, b_router, w1_e, w3_e, w2_e, w1_s, w3_s, w2_s, g_final):
    nb, s, d = x_prompt.shape
    db = x_sample.shape[0]
    w = conv_w.shape[-1]
    n_exp = w_router.shape[-1]
    l = 0
    row = lambda v: v.reshape(1, -1)

    m = _mod_call(jnp.concatenate([c_prompt, c_sample], axis=0), w_mod[l], row(b_mod[l]))
    mods_p = [m[:nb, i * d:(i + 1) * d].reshape(nb, 1, d) for i in range(6)]
    mods_s = [m[nb:, i * d:(i + 1) * d] for i in range(6)]

    w_in_b = w_in[l].astype(BF16)
    w_out_b = w_out[l].astype(BF16)
    wa_b = w_gate_a[l].astype(BF16)
    wx_b = w_gate_x[l].astype(BF16)
    wr2 = jnp.concatenate([w_router[l], w_router[l]], axis=1)
    wr_hi = wr2.astype(BF16)
    wr_lo = (wr2 - wr_hi.astype(F32)).astype(BF16)
    br2 = row(jnp.concatenate([b_router[l], b_router[l]]).astype(F32))
    w1s_b, w3s_b, w2s_b = w1_s[l].astype(BF16), w3_s[l].astype(BF16), w2_s[l].astype(BF16)
    lru_args = (conv_w[l], row(conv_b[l]), wa_b, row(b_gate_a[l]), wx_b, row(b_gate_x[l]), row(lru_lambda[l]))

    def mix_and_route(yl, ya, x, mods, rows_per_mod, tm):
        return _out_proj_call(yl, ya, x, row(g_out_lru[l]), row(g_out_att[l]), w_out_b, mods[2], mods[4], mods[3],
                              mods[5], row(g_norm2[l]), wr_hi, wr_lo, br2, w1s_b, w3s_b, w2s_b, rows_per_mod, tm)

    xp = x_prompt.reshape(nb * s, d)
    proj_p = _in_proj_call(xp, row(g_norm1[l]), mods_p[1], mods_p[0], w_in_b, s, 512)
    yl_p, hl_p = _lru_prompt_call(proj_p, nb, *lru_args)
    ya_p = _attn_prompt_call(proj_p, nb, sb_bias[l])
    xs_p, h2_p, gate_p, rank_p, cum_p = mix_and_route(yl_p.reshape(nb * s, w), ya_p, xp, mods_p, s, 256)

    xsm = x_sample.reshape(db, d)
    proj_s = _in_proj_call(xsm, row(g_norm1[l]), mods_s[1], mods_s[0], w_in_b, 1, db)
    yl_s, hl_s = _lru_sample_call(proj_s, state_conv[l].reshape(db, (CONV_W - 1) * w), state_lru_h[l], *lru_args)
    ya_s = _attn_sample_call(proj_s[2], cache_k[l], cache_v[l], page_table, sb_bias[l])
    xs_s, h2_s, gate_s, rank_s, cum_s = mix_and_route(yl_s, ya_s, xsm, mods_s, 1, db)

    block_e, block_valid, block_idx, n_rows, pos_p, w8_p, pos_s, w8_s = _plan(
        rank_p, cum_p, gate_p, rank_s, cum_s, gate_s, n_exp)
    x_sorted = _scatter_call(h2_p, pos_p, n_rows)
    x_sorted = _scatter_call(h2_s, pos_s, n_rows, xs_prev=x_sorted, tb=db)
    y_sorted = _experts_call(block_e, block_valid, block_idx, x_sorted, w1_e[l], w3_e[l], w2_e[l])
    y_p = _combine_call(pos_p, w8_p, xs_p, mods_p[5], row(g_final), y_sorted, s)
    y_s = _combine_call(pos_s, w8_s, xs_s, mods_s[5], row(g_final), y_sorted, 1)

    nh = w // HEAD_DIM
    tail = CONV_W - 1
    k_p = proj_p[3].reshape(1, nb, s, nh, HEAD_DIM)
    v_p = proj_p[4].reshape(1, nb, s, nh, HEAD_DIM)
    conv_p = proj_p[0].reshape(nb, s, w)[:, s - tail:, :][None]
    k_s = proj_s[3].reshape(1, db, 1, nh, HEAD_DIM)
    v_s = proj_s[4].reshape(1, db, 1, nh, HEAD_DIM)
    conv_s = jnp.concatenate([state_conv[l][:, 1:, :], proj_s[0][:, None, :]], axis=1)[None]
    return (y_p.reshape(nb, s, d), y_s.reshape(db, 1, d), k_p, v_p, hl_p[None], conv_p,
            k_s, v_s, hl_s[None], conv_s)
```

```python
import functools

import jax
import jax.numpy as jnp
from jax import lax
from jax.experimental import pallas as pl
from jax.experimental.pallas import tpu as pltpu

F32 = jnp.float32
BF16 = jnp.bfloat16

EPS = 1e-6
LRU_C = 8.0
CONV_W = 4
LRU_BLOCKS = 8
N_GROUPS = 8
TOPK_GROUPS = 4
TOP_K = 8
ROUTED_SCALE = 2.5
LANES = 128
HEAD_DIM = 128
MOE_BM = 256
VMEM_LIMIT = 56 * 1024 * 1024


def _cparams(sem):
    return pltpu.CompilerParams(dimension_semantics=sem, vmem_limit_bytes=VMEM_LIMIT)


def _dot(a, b):
    return jnp.dot(a, b, preferred_element_type=F32)


def _dot_nt(a, b):
    return lax.dot_general(a, b, (((1,), (1,)), ((), ())), preferred_element_type=F32)


def _split(x):
    hi = x.astype(BF16)
    lo = (x - hi.astype(F32)).astype(BF16)
    return hi, lo


def _softplus(z):
    return jnp.maximum(z, 0.0) + jnp.log(1.0 + jnp.exp(-jnp.abs(z)))


def _sigmoid(z):
    return 1.0 / (1.0 + jnp.exp(-z))


def _silu(z):
    return z * _sigmoid(z)


def _gelu_tanh(x):
    return 0.5 * x * (1.0 + jnp.tanh(0.7978845608028654 * (x + 0.044715 * (x * x * x))))


def _rms(x, g):
    return x * lax.rsqrt(jnp.mean(x * x, axis=-1, keepdims=True) + EPS) * g


def _pack_pairs(x):
    half = x.shape[1] // 2
    lo = lax.bitcast_convert_type(x[:, :half].astype(BF16).astype(F32), jnp.uint32)
    hi = lax.bitcast_convert_type(x[:, half:].astype(BF16).astype(F32), jnp.uint32)
    return (lo >> 16) | hi


def _unpack_pairs(w):
    lo = lax.bitcast_convert_type(w << 16, F32)
    hi = lax.bitcast_convert_type(w & jnp.uint32(0xFFFF0000), F32)
    return lo, hi


def _mod_kernel(c_ref, w_ref, b_ref, o_ref):
    a = _silu(c_ref[...]).astype(BF16)
    o_ref[...] = _dot(a, w_ref[...].astype(BF16)) + b_ref[...]


def _mod_call(c_all, w_mod, b_mod, tn=512):
    m, d = c_all.shape
    n = w_mod.shape[1]
    return pl.pallas_call(
        _mod_kernel,
        grid=(n // tn,),
        in_specs=[pl.BlockSpec((m, d), lambda j: (0, 0)),
                  pl.BlockSpec((d, tn), lambda j: (0, j)),
                  pl.BlockSpec((1, tn), lambda j: (0, j))],
        out_specs=pl.BlockSpec((m, tn), lambda j: (0, j)),
        out_shape=jax.ShapeDtypeStruct((m, n), F32),
        compiler_params=_cparams(("arbitrary",)),
        name="mod",
    )(c_all, w_mod, b_mod)


def _mod_specs(mod, tm, d, rows_per_mod, n_grid_axes):
    if mod.ndim == 3:
        tiles = rows_per_mod // tm
        if n_grid_axes == 1:
            return pl.BlockSpec((None, 1, d), lambda i: (i // tiles, 0, 0))
        return pl.BlockSpec((None, 1, d), lambda i, j: (i // tiles, 0, 0))
    if n_grid_axes == 1:
        return pl.BlockSpec((tm, d), lambda i: (i, 0))
    return pl.BlockSpec((tm, d), lambda i, j: (i, 0))


def _in_proj_kernel(x_ref, g_ref, sc_ref, sh_ref, w_ref, o_ref, h_scr):
    @pl.when(pl.program_id(1) == 0)
    def _():
        h = _rms(x_ref[...], g_ref[...]) * (1.0 + sc_ref[...]) + sh_ref[...]
        h_scr[...] = h.astype(BF16)

    o_ref[...] = _dot(h_scr[...], w_ref[...])


def _in_proj_call(x, g, sc, sh, w_in, rows_per_mod, tm):
    t, d = x.shape
    w = w_in.shape[1] // 5
    return pl.pallas_call(
        _in_proj_kernel,
        grid=(t // tm, 5),
        in_specs=[pl.BlockSpec((tm, d), lambda i, j: (i, 0)),
                  pl.BlockSpec((1, d), lambda i, j: (0, 0)),
                  _mod_specs(sc, tm, d, rows_per_mod, 2),
                  _mod_specs(sh, tm, d, rows_per_mod, 2),
                  pl.BlockSpec((d, w), lambda i, j: (0, j))],
        out_specs=pl.BlockSpec((None, tm, w), lambda i, j: (j, i, 0)),
        out_shape=jax.ShapeDtypeStruct((5, t, w), F32),
        scratch_shapes=[pltpu.VMEM((tm, d), BF16)],
        compiler_params=_cparams(("arbitrary", "arbitrary")),
        name="in_proj",
    )(x, g, sc, sh, w_in)


def _lru_gates(xc, wa_ref, ba, wx_ref, bx, sp):
    xb = xc.astype(BF16)
    blk = xc.shape[1] // LRU_BLOCKS
    za = jnp.concatenate([_dot(xb[:, n * blk:(n + 1) * blk], wa_ref[n]) for n in range(LRU_BLOCKS)], axis=1)
    zx = jnp.concatenate([_dot(xb[:, n * blk:(n + 1) * blk], wx_ref[n]) for n in range(LRU_BLOCKS)], axis=1)
    r = _sigmoid(za + ba)
    i = _sigmoid(zx + bx)
    log_a = -LRU_C * r * sp
    a = jnp.exp(log_a)
    u = jnp.sqrt(1.0 - jnp.exp(2.0 * log_a)) * (i * xc)
    return a, u


def _lru_prompt_kernel(xl_ref, gt_ref, cw_ref, cb_ref, wa_ref, ba_ref, wx_ref, bx_ref, lam_ref,
                       y_ref, hl_ref, xp_scr, h_scr, a_scr, u_scr):
    nb, tc, w = xl_ref.shape
    pad = xp_scr.shape[1] - tc
    nc = w // LANES

    @pl.when(pl.program_id(0) == 0)
    def _():
        xp_scr[:, 0:pad, :] = jnp.zeros((nb, pad, w), F32)
        h_scr[...] = jnp.zeros_like(h_scr)

    sp = _softplus(-lam_ref[...])
    cw = cw_ref[...]
    for b in range(nb):
        x = xl_ref[b]
        xp_scr[b, pad:pad + tc, :] = x
        xc = cw[CONV_W - 1:CONV_W, :] * x + cb_ref[...]
        for j in range(CONV_W - 1):
            off = pad - (CONV_W - 1) + j
            xc = xc + cw[j:j + 1, :] * xp_scr[b, off:off + tc, :]
        a, u = _lru_gates(xc, wa_ref, ba_ref[...], wx_ref, bx_ref[...], sp)
        for c in range(nc):
            a_scr[c, b * tc:(b + 1) * tc, :] = a[:, c * LANES:(c + 1) * LANES]
            u_scr[c, b * tc:(b + 1) * tc, :] = u[:, c * LANES:(c + 1) * LANES]
        xp_scr[b, 0:pad, :] = x[tc - pad:tc, :]

    h = [h_scr[:, c * LANES:(c + 1) * LANES] for c in range(nc)]
    for t in range(tc):
        rows = pl.ds(t, nb, stride=tc)
        for c in range(nc):
            h[c] = a_scr[c, rows, :] * h[c] + u_scr[c, rows, :]
            u_scr[c, rows, :] = h[c]
    h = jnp.concatenate(h, axis=1)
    h_scr[...] = h
    hl_ref[...] = h

    for b in range(nb):
        hs = jnp.concatenate([u_scr[c, b * tc:(b + 1) * tc, :] for c in range(nc)], axis=1)
        y_ref[b] = (_gelu_tanh(gt_ref[b]) * hs).astype(BF16)


def _lru_prompt_call(proj5, nb, cw, cb, wa, ba, wx, bx, lam, tc=128):
    _, t, w = proj5.shape
    s = t // nb
    p4 = proj5.reshape(5, nb, s, w)
    full = lambda shape: pl.BlockSpec(shape, lambda c: (0,) * len(shape))
    return pl.pallas_call(
        _lru_prompt_kernel,
        grid=(s // tc,),
        in_specs=[pl.BlockSpec((None, nb, tc, w), lambda c: (0, 0, c, 0)),
                  pl.BlockSpec((None, nb, tc, w), lambda c: (1, 0, c, 0)),
                  full(cw.shape), full(cb.shape), full(wa.shape), full(ba.shape),
                  full(wx.shape), full(bx.shape), full(lam.shape)],
        out_specs=[pl.BlockSpec((nb, tc, w), lambda c: (0, c, 0)),
                   pl.BlockSpec((nb, w), lambda c: (0, 0))],
        out_shape=[jax.ShapeDtypeStruct((nb, s, w), BF16),
                   jax.ShapeDtypeStruct((nb, w), F32)],
        scratch_shapes=[pltpu.VMEM((nb, tc + 8, w), F32),
                        pltpu.VMEM((nb, w), F32),
                        pltpu.VMEM((w // LANES, nb * tc, LANES), F32),
                        pltpu.VMEM((w // LANES, nb * tc, LANES), F32)],
        compiler_params=_cparams(("arbitrary",)),
        name="lru_prompt",
    )(p4, p4, cw, cb, wa, ba, wx, bx, lam)


def _lru_sample_kernel(xl_ref, gt_ref, cs_ref, h0_ref, cw_ref, cb_ref, wa_ref, ba_ref, wx_ref, bx_ref,
                       lam_ref, y_ref, hl_ref):
    w = xl_ref.shape[1]
    x = xl_ref[...]
    cw = cw_ref[...]
    xc = cw[CONV_W - 1:CONV_W, :] * x + cb_ref[...]
    for j in range(CONV_W - 1):
        xc = xc + cw[j:j + 1, :] * cs_ref[:, j * w:(j + 1) * w]
    a, u = _lru_gates(xc, wa_ref, ba_ref[...], wx_ref, bx_ref[...], _softplus(-lam_ref[...]))
    h = a * h0_ref[...] + u
    hl_ref[...] = h
    y_ref[...] = (_gelu_tanh(gt_ref[...]) * h).astype(BF16)


def _lru_sample_call(proj5, conv_state, h0, cw, cb, wa, ba, wx, bx, lam):
    _, t, w = proj5.shape
    full = lambda shape: pl.BlockSpec(shape, lambda c: (0,) * len(shape))
    return pl.pallas_call(
        _lru_sample_kernel,
        grid=(1,),
        in_specs=[pl.BlockSpec((None, t, w), lambda c: (0, 0, 0)),
                  pl.BlockSpec((None, t, w), lambda c: (1, 0, 0)),
                  full(conv_state.shape), full(h0.shape),
                  full(cw.shape), full(cb.shape), full(wa.shape), full(ba.shape),
                  full(wx.shape), full(bx.shape), full(lam.shape)],
        out_specs=[full((t, w)), full((t, w))],
        out_shape=[jax.ShapeDtypeStruct((t, w), BF16), jax.ShapeDtypeStruct((t, w), F32)],
        compiler_params=_cparams(("arbitrary",)),
        name="lru_sample",
    )(proj5, proj5, conv_state, h0, cw, cb, wa, ba, wx, bx, lam)


def _strict_upper(n):
    j = lax.broadcasted_iota(jnp.int32, (n, n), 0)
    s = lax.broadcasted_iota(jnp.int32, (n, n), 1)
    return jnp.where(j > s, 1.0, 0.0).astype(BF16)


def _sb_block(q, k, v, bias, upper, r_run, acc, scale, causal):
    z = _dot_nt(q, k) * scale + bias
    sp = _softplus(z)
    lk = -sp
    if causal is not None:
        lk = jnp.where(causal, lk, 0.0)
    hi, lo = _split(lk)
    later = _dot(hi, upper) + _dot(lo, upper)
    p = jnp.exp((z - sp) + later + r_run)
    if causal is not None:
        p = jnp.where(causal, p, 0.0)
    acc = acc + _dot(p.astype(BF16), v)
    r_run = r_run + jnp.sum(lk, axis=1, keepdims=True)
    return r_run, acc


def _attn_prompt_kernel(q_ref, k_ref, v_ref, b_ref, o_ref):
    tq = q_ref.shape[0]
    hd = HEAD_DIM
    heads = q_ref.shape[1] // hd
    qi = pl.program_id(2)
    scale = hd ** -0.5
    upper = _strict_upper(tq)
    row = lax.broadcasted_iota(jnp.int32, (tq, tq), 0)
    col = lax.broadcasted_iota(jnp.int32, (tq, tq), 1)
    qs = [q_ref[:, h * hd:(h + 1) * hd].astype(BF16) for h in range(heads)]
    biases = [b_ref[h] for h in range(heads)]

    def blocks(kb, carry, causal):
        start = pl.multiple_of(kb * tq, tq)
        out = []
        for h in range(heads):
            k = k_ref[pl.ds(start, tq), h * hd:(h + 1) * hd].astype(BF16)
            v = v_ref[pl.ds(start, tq), h * hd:(h + 1) * hd].astype(BF16)
            out.append(_sb_block(qs[h], k, v, biases[h], upper, carry[h][0], carry[h][1], scale, causal))
        return tuple(out)

    init = tuple((jnp.zeros((tq, 1), F32), jnp.zeros((tq, hd), F32)) for _ in range(heads))
    carry = blocks(qi, init, col < row)
    carry = lax.fori_loop(0, qi, lambda n, c: blocks(qi - 1 - n, c, None), carry)
    for h in range(heads):
        o_ref[:, h * hd:(h + 1) * hd] = carry[h][1].astype(BF16)


def _attn_prompt_call(proj5, nb, sb_bias, tq=256, heads=2):
    _, t, w = proj5.shape
    s = t // nb
    nh = w // HEAD_DIM
    nq = s // tq
    hw = heads * HEAD_DIM
    bias = jnp.broadcast_to(sb_bias.astype(F32)[:, None, None], (nh, 1, tq))
    return pl.pallas_call(
        _attn_prompt_kernel,
        grid=(nb, nh // heads, nq),
        in_specs=[pl.BlockSpec((None, tq, hw), lambda b, h, i: (2, b * nq + i, h)),
                  pl.BlockSpec((None, s, hw), lambda b, h, i: (3, b, h)),
                  pl.BlockSpec((None, s, hw), lambda b, h, i: (4, b, h)),
                  pl.BlockSpec((heads, 1, tq), lambda b, h, i: (h, 0, 0))],
        out_specs=pl.BlockSpec((tq, hw), lambda b, h, i: (b * nq + i, h)),
        out_shape=jax.ShapeDtypeStruct((t, w), BF16),
        compiler_params=_cparams(("arbitrary", "arbitrary", "arbitrary")),
        name="attn_prompt",
    )(proj5, proj5, proj5, bias)


def _attn_sample_kernel(pt_ref, q_ref, b_ref, *refs, pages_per_step):
    k_refs = refs[:pages_per_step]
    v_refs = refs[pages_per_step:2 * pages_per_step]
    o_ref, r_scr, acc_scr = refs[2 * pages_per_step:]
    nh, hd = q_ref.shape
    page = k_refs[0].shape[0] // nh
    step = pl.program_id(1)

    @pl.when(step == 0)
    def _():
        r_scr[...] = jnp.zeros_like(r_scr)
        acc_scr[...] = jnp.zeros_like(acc_scr)

    head = lax.broadcasted_iota(jnp.int32, (nh, hd), 0)
    q = q_ref[...].astype(BF16)
    upper = _strict_upper(page)
    bias = b_ref[...]
    logits, within, totals = [], [], []
    for g in range(pages_per_step):
        z = jnp.zeros((nh, page), F32)
        for h in range(nh):
            kh = k_refs[g][pl.ds(h, page, stride=nh), :].astype(BF16)
            z = jnp.where(head == h, _dot_nt(q, kh), z)
        z = z * (hd ** -0.5) + bias
        sp = _softplus(z)
        hi, lo = _split(-sp)
        logits.append(z - sp)
        within.append(_dot(hi, upper) + _dot(lo, upper))
        totals.append(jnp.sum(-sp, axis=1, keepdims=True))
    r_run = r_scr[...]
    acc = acc_scr[...]
    for g in range(pages_per_step):
        p = jnp.exp(logits[g] + within[g] + r_run).astype(BF16)
        for h in range(nh):
            vh = v_refs[g][pl.ds(h, page, stride=nh), :].astype(BF16)
            acc = acc + jnp.where(head == h, _dot(p, vh), 0.0)
        r_run = r_run + totals[g]
    r_scr[...] = r_run
    acc_scr[...] = acc

    @pl.when(step == pl.num_programs(1) - 1)
    def _():
        o_ref[...] = acc.astype(BF16)


def _attn_sample_call(q, cache_k, cache_v, page_table, sb_bias, pages_per_step=8):
    db, w = q.shape
    n_pool, page, nh, hd = cache_k.shape
    n_pages = page_table.shape[1]
    ck = cache_k.reshape(n_pool, page * nh, hd)
    cv = cache_v.reshape(n_pool, page * nh, hd)
    bias = jnp.broadcast_to(sb_bias.astype(F32)[:, None], (nh, page))

    def page_spec(g):
        return pl.BlockSpec((None, page * nh, hd),
                            lambda b, s, pt: (pt[b, n_pages - 1 - (s * pages_per_step + g)], 0, 0))

    specs = [page_spec(g) for g in range(pages_per_step)]
    return pl.pallas_call(
        functools.partial(_attn_sample_kernel, pages_per_step=pages_per_step),
        grid_spec=pltpu.PrefetchScalarGridSpec(
            num_scalar_prefetch=1,
            grid=(db, n_pages // pages_per_step),
            in_specs=[pl.BlockSpec((None, nh, hd), lambda b, s, pt: (b, 0, 0)),
                      pl.BlockSpec((nh, page), lambda b, s, pt: (0, 0))] + specs + specs,
            out_specs=pl.BlockSpec((None, nh, hd), lambda b, s, pt: (b, 0, 0)),
            scratch_shapes=[pltpu.VMEM((nh, 1), F32), pltpu.VMEM((nh, hd), F32)]),
        out_shape=jax.ShapeDtypeStruct((db, nh, hd), BF16),
        compiler_params=_cparams(("arbitrary", "arbitrary")),
        name="attn_sample",
    )(page_table, q.reshape(db, nh, hd), bias, *([ck] * pages_per_step), *([cv] * pages_per_step)).reshape(db, w)


def _xor_partner(x, lane, s):
    n = x.shape[1]
    up = pltpu.roll(x, n - s, axis=1)
    dn = pltpu.roll(x, s, axis=1)
    return jnp.where((lane & s) == 0, up, dn)


def _route(h2, wr_hi_ref, wr_lo_ref, br):
    tm = h2.shape[0]
    n_exp = LANES // 2
    per_group = n_exp // N_GROUPS
    h_hi, h_lo = _split(h2)
    logits = _dot(h_hi, wr_hi_ref[...]) + _dot(h_lo, wr_hi_ref[...]) + _dot(h_hi, wr_lo_ref[...])
    scores = _sigmoid(logits)
    biased = scores + br
    lane = lax.broadcasted_iota(jnp.int32, (tm, LANES), 1)
    e = lane & (n_exp - 1)
    g = e >> (per_group.bit_length() - 1)
    m1 = biased
    m2 = jnp.full_like(biased, -jnp.inf)
    for s in (1, 2, 4):
        p1 = _xor_partner(m1, lane, s)
        p2 = _xor_partner(m2, lane, s)
        m1, m2 = jnp.maximum(m1, p1), jnp.maximum(jnp.minimum(m1, p1), jnp.maximum(m2, p2))
    gs = m1 + m2
    g_rank = jnp.zeros((tm, LANES), jnp.int32)
    for k in range(1, N_GROUPS):
        other = pltpu.roll(gs, per_group * k, axis=1)
        beats = (other > gs) | ((other == gs) & (g >= k))
        g_rank = g_rank + beats.astype(jnp.int32)
    masked = jnp.where(g_rank < TOPK_GROUPS, biased, -jnp.inf)
    e_rank = jnp.zeros((tm, LANES), jnp.int32)
    for k in range(1, n_exp):
        other = pltpu.roll(masked, k, axis=1)
        beats = (other > masked) | ((other == masked) & (e >= k))
        e_rank = e_rank + beats.astype(jnp.int32)
    first = lane < n_exp
    sel = (e_rank < TOP_K) & first
    wsel = jnp.where(sel, scores, 0.0)
    gate = wsel / jnp.sum(wsel, axis=1, keepdims=True) * ROUTED_SCALE
    return gate, jnp.where(first, e_rank, n_exp), sel


def _out_proj_kernel(yl_ref, ya_ref, x_ref, gl_ref, ga_ref, wl_ref, wa_ref, g1_ref, sc_ref, sh_ref, g2_ref,
                     gn_ref, wrh_ref, wrl_ref, br_ref, w1_ref, w3_ref, w2_ref,
                     xs_ref, h2_ref, gate_ref, rank_ref, cum_ref, cnt_scr):
    tm = x_ref.shape[0]

    @pl.when(pl.program_id(0) == 0)
    def _():
        cnt_scr[...] = jnp.zeros_like(cnt_scr)

    yl = _rms(yl_ref[...].astype(F32), gl_ref[...]).astype(BF16)
    ya = _rms(ya_ref[...].astype(F32), ga_ref[...]).astype(BF16)
    mixed = _dot(yl, wl_ref[...]) + _dot(ya, wa_ref[...])
    x1 = x_ref[...] + g1_ref[...] * mixed
    h2 = _rms(x1, gn_ref[...]) * (1.0 + sc_ref[...]) + sh_ref[...]
    h2_ref[...] = _pack_pairs(h2)

    gate, e_rank, sel = _route(h2, wrh_ref, wrl_ref, br_ref[...])
    gate_ref[...] = gate
    rank_ref[...] = e_rank
    i = lax.broadcasted_iota(jnp.int32, (tm, tm), 0)
    j = lax.broadcasted_iota(jnp.int32, (tm, tm), 1)
    lower = jnp.where(j < i, 1.0, 0.0).astype(BF16)
    selb = jnp.where(sel, 1.0, 0.0).astype(BF16)
    cum = _dot(lower, selb) + cnt_scr[...]
    cum_ref[...] = cum.astype(jnp.int32)
    cnt_scr[...] = cnt_scr[...] + jnp.sum(selb.astype(F32), axis=0, keepdims=True)

    hb = h2.astype(BF16)
    act = (_silu(_dot(hb, w1_ref[...])) * _dot(hb, w3_ref[...])).astype(BF16)
    shared = _dot(act, w2_ref[...])
    xs_ref[...] = x1 + g2_ref[...] * shared


def _out_proj_call(yl, ya, x, gl, ga, w_out, g1, sc2, sh2, g2, gn, wr_hi, wr_lo, br, w1s, w3s, w2s,
                   rows_per_mod, tm):
    t, d = x.shape
    w = yl.shape[1]
    full = lambda shape: pl.BlockSpec(shape, lambda i: (0,) * len(shape))
    mod = lambda m: _mod_specs(m, tm, d, rows_per_mod, 1)
    row = lambda n: pl.BlockSpec((tm, n), lambda i: (i, 0))
    return pl.pallas_call(
        _out_proj_kernel,
        grid=(t // tm,),
        in_specs=[row(w), row(w), row(d), full(gl.shape), full(ga.shape),
                  pl.BlockSpec((w, d), lambda i: (0, 0)), pl.BlockSpec((w, d), lambda i: (1, 0)),
                  mod(g1), mod(sc2), mod(sh2), mod(g2), full(gn.shape),
                  full(wr_hi.shape), full(wr_lo.shape), full(br.shape),
                  full(w1s.shape), full(w3s.shape), full(w2s.shape)],
        out_specs=[row(d), row(d // 2), row(LANES), row(LANES), row(LANES)],
        out_shape=[jax.ShapeDtypeStruct((t, d), F32), jax.ShapeDtypeStruct((t, d // 2), jnp.uint32),
                   jax.ShapeDtypeStruct((t, LANES), F32), jax.ShapeDtypeStruct((t, LANES), jnp.int32),
                   jax.ShapeDtypeStruct((t, LANES), jnp.int32)],
        scratch_shapes=[pltpu.VMEM((1, LANES), F32)],
        compiler_params=_cparams(("arbitrary",)),
        name="out_proj",
    )(yl, ya, x, gl, ga, w_out, w_out, g1, sc2, sh2, g2, gn, wr_hi, wr_lo, br, w1s, w3s, w2s)


def _row_copy(src, dst, src_row, dst_row, sem):
    return pltpu.make_async_copy(src.at[pl.ds(src_row, 1), :], dst.at[pl.ds(dst_row, 1), :], sem)


def _scatter_kernel(pos_ref, h_ref, *refs):
    xs_ref, sem = refs[-2], refs[-1]
    tb = h_ref.shape[0]

    def issue(t, c):
        for j in range(TOP_K):
            _row_copy(h_ref, xs_ref, t, pos_ref[t * TOP_K + j], sem).start()
        return c

    lax.fori_loop(0, tb, issue, 0)

    def drain(t, c):
        for j in range(TOP_K):
            _row_copy(h_ref, xs_ref, t, pos_ref[t * TOP_K + j], sem).wait()
        return c

    lax.fori_loop(0, tb, drain, 0)


def _scatter_call(h2, pos_flat, n_rows, xs_prev=None, tb=256):
    t, d = h2.shape
    any_spec = pl.BlockSpec(memory_space=pl.ANY)
    in_specs = [pl.BlockSpec((tb * TOP_K,), lambda i: (i,), memory_space=pltpu.SMEM),
                pl.BlockSpec((tb, d), lambda i: (i, 0))]
    args = [pos_flat, h2]
    aliases = {}
    if xs_prev is not None:
        in_specs.append(any_spec)
        args.append(xs_prev)
        aliases = {2: 0}
    return pl.pallas_call(
        _scatter_kernel,
        grid=(t // tb,),
        in_specs=in_specs,
        out_specs=any_spec,
        out_shape=jax.ShapeDtypeStruct((n_rows, d), h2.dtype),
        scratch_shapes=[pltpu.SemaphoreType.DMA(())],
        input_output_aliases=aliases,
        compiler_params=_cparams(("arbitrary",)),
        name="moe_scatter",
    )(*args)


def _experts_kernel(be_ref, bv_ref, bi_ref, x_ref, w1_ref, w3_ref, w2_ref, o_ref, w1_scr, w3_scr, w2_scr):
    i = pl.program_id(0)
    prev = be_ref[jnp.maximum(i - 1, 0)]

    @pl.when((i == 0) | (be_ref[i] != prev))
    def _():
        w1_scr[...] = w1_ref[...].astype(BF16)
        w3_scr[...] = w3_ref[...].astype(BF16)
        w2_scr[...] = w2_ref[...].astype(BF16)

    valid = bv_ref[i]

    @pl.when(valid > 0)
    def _():
        bm, half = x_ref.shape
        row = lax.broadcasted_iota(jnp.int32, (bm, 1), 0)
        x_lo, x_hi = _unpack_pairs(jnp.where(row < valid, x_ref[...], jnp.uint32(0)))
        x_lo, x_hi = x_lo.astype(BF16), x_hi.astype(BF16)
        h1 = _dot(x_lo, w1_scr[0:half, :]) + _dot(x_hi, w1_scr[half:2 * half, :])
        h3 = _dot(x_lo, w3_scr[0:half, :]) + _dot(x_hi, w3_scr[half:2 * half, :])
        act = (_silu(h1) * h3).astype(BF16)
        o_ref[...] = _pack_pairs(_dot(act, w2_scr[...]))


def _experts_call(block_e, block_valid, block_idx, xs, w1, w3, w2):
    n_rows, d = xs.shape
    n_exp, _, ff = w1.shape
    d_full = w1.shape[1]
    bm = MOE_BM
    return pl.pallas_call(
        _experts_kernel,
        grid_spec=pltpu.PrefetchScalarGridSpec(
            num_scalar_prefetch=3,
            grid=(n_rows // bm,),
            in_specs=[pl.BlockSpec((bm, d), lambda i, be, bv, bi: (bi[i], 0)),
                      pl.BlockSpec((None, d_full, ff), lambda i, be, bv, bi: (be[i], 0, 0)),
                      pl.BlockSpec((None, d_full, ff), lambda i, be, bv, bi: (be[i], 0, 0)),
                      pl.BlockSpec((None, ff, d_full), lambda i, be, bv, bi: (be[i], 0, 0))],
            out_specs=pl.BlockSpec((bm, d), lambda i, be, bv, bi: (bi[i], 0)),
            scratch_shapes=[pltpu.VMEM((d_full, ff), BF16), pltpu.VMEM((d_full, ff), BF16),
                            pltpu.VMEM((ff, d_full), BF16)]),
        out_shape=jax.ShapeDtypeStruct((n_rows, d), jnp.uint32),
        compiler_params=_cparams(("arbitrary",)),
        name="moe_experts",
    )(block_e, block_valid, block_idx, xs, w1, w3, w2)


def _combine_kernel(pos_ref, w_ref, xs_ref, g2_ref, gf_ref, y_hbm, o_ref, buf, sem):
    tb = xs_ref.shape[0]

    def row_copy(t, j):
        return pltpu.make_async_copy(y_hbm.at[pl.ds(pos_ref[t * TOP_K + j], 1), :],
                                     buf.at[j, pl.ds(t, 1), :], sem)

    def issue(t, c):
        for j in range(TOP_K):
            row_copy(t, j).start()
        return c

    lax.fori_loop(0, tb, issue, 0)

    def drain(t, c):
        for j in range(TOP_K):
            row_copy(t, j).wait()
        return c

    lax.fori_loop(0, tb, drain, 0)

    wts = w_ref[...]
    r_lo = jnp.zeros(buf.shape[1:], F32)
    r_hi = jnp.zeros(buf.shape[1:], F32)
    for j in range(TOP_K):
        y_lo, y_hi = _unpack_pairs(buf[j])
        r_lo = r_lo + wts[:, j:j + 1] * y_lo
        r_hi = r_hi + wts[:, j:j + 1] * y_hi
    routed = jnp.concatenate([r_lo, r_hi], axis=1)
    x2 = xs_ref[...] + g2_ref[...] * routed
    o_ref[...] = _rms(x2, gf_ref[...])


def _combine_call(pos_flat, w8, xs, g2, gf, y_sorted, rows_per_mod, tb=128):
    t, d = xs.shape
    tb = min(tb, t)
    return pl.pallas_call(
        _combine_kernel,
        grid=(t // tb,),
        in_specs=[pl.BlockSpec((tb * TOP_K,), lambda i: (i,), memory_space=pltpu.SMEM),
                  pl.BlockSpec((tb, TOP_K), lambda i: (i, 0)),
                  pl.BlockSpec((tb, d), lambda i: (i, 0)),
                  _mod_specs(g2, tb, d, rows_per_mod, 1),
                  pl.BlockSpec((1, d), lambda i: (0, 0)),
                  pl.BlockSpec(memory_space=pl.ANY)],
        out_specs=pl.BlockSpec((tb, d), lambda i: (i, 0)),
        out_shape=jax.ShapeDtypeStruct((t, d), F32),
        scratch_shapes=[pltpu.VMEM((TOP_K, tb, d // 2), jnp.uint32), pltpu.SemaphoreType.DMA(())],
        compiler_params=_cparams(("arbitrary",)),
        name="moe_combine",
    )(pos_flat, w8, xs, g2, gf, y_sorted)


def _plan(e_rank_p, cum_p, gate_p, e_rank_s, cum_s, gate_s, n_exp):
    bm = MOE_BM
    sel_p = e_rank_p[:, :n_exp] < TOP_K
    sel_s = e_rank_s[:, :n_exp] < TOP_K
    cnt_p = jnp.sum(sel_p, axis=0, dtype=jnp.int32)
    counts = cnt_p + jnp.sum(sel_s, axis=0, dtype=jnp.int32)
    padded = (counts + bm - 1) // bm * bm
    pad_end = jnp.cumsum(padded)
    pad_start = pad_end - padded
    n_assign = (e_rank_p.shape[0] + e_rank_s.shape[0]) * TOP_K
    n_blocks = -(-n_assign // bm) + n_exp
    blk_start = jnp.arange(n_blocks, dtype=jnp.int32) * bm
    block_e = jnp.minimum(jnp.sum(pad_end[None, :] <= blk_start[:, None], axis=1, dtype=jnp.int32), n_exp - 1)
    block_valid = jnp.clip(pad_start[block_e] + counts[block_e] - blk_start, 0, bm).astype(jnp.int32)
    n_used = pad_end[-1] // bm
    block_idx = jnp.minimum(jnp.arange(n_blocks, dtype=jnp.int32), jnp.maximum(n_used - 1, 0)).astype(jnp.int32)
    slots = jnp.arange(TOP_K, dtype=jnp.int32)[None, :, None]

    def compact(e_rank, cum, gate, offset):
        dest = pad_start[None, :] + offset[None, :] + cum[:, :n_exp]
        hit = e_rank[:, None, :n_exp] == slots
        pos = jnp.sum(jnp.where(hit, dest[:, None, :], 0), axis=-1, dtype=jnp.int32)
        wts = jnp.sum(jnp.where(hit, gate[:, None, :n_exp], 0.0), axis=-1)
        return pos.reshape(-1), wts

    pos_p, w_p = compact(e_rank_p, cum_p, gate_p, jnp.zeros_like(cnt_p))
    pos_s, w_s = compact(e_rank_s, cum_s, gate_s, cnt_p)
    return block_e, block_valid, block_idx, n_blocks * bm, pos_p, w_p, pos_s, w_s


def kernel(x_prompt, x_sample, c_prompt, c_sample, cache_k, cache_v, page_table, state_lru_h, state_conv, w_mod, b_mod, g_norm1, g_norm2, w_in, conv_w, conv_b, w_gate_a, b_gate_a, w_gate_x, b_gate_x, lru_lambda, sb_bias, g_out_lru, g_out_att, w_out, w_router, b_router, w1_e, w3_e, w2_e, w1_s, w3_s, w2_s, g_final):
    nb, s, d = x_prompt.shape
    db = x_sample.shape[0]
    w = conv_w.shape[-1]
    n_exp = w_router.shape[-1]
    l = 0
    row = lambda v: v.reshape(1, -1)

    m = _mod_call(jnp.concatenate([c_prompt, c_sample], axis=0), w_mod[l], row(b_mod[l]))
    mods_p = [m[:nb, i * d:(i + 1) * d].reshape(nb, 1, d) for i in range(6)]
    mods_s = [m[nb:, i * d:(i + 1) * d] for i in range(6)]

    w_in_b = w_in[l].astype(BF16)
    w_out_b = w_out[l].astype(BF16)
    wa_b = w_gate_a[l].astype(BF16)
    wx_b = w_gate_x[l].astype(BF16)
    wr2 = jnp.concatenate([w_router[l], w_router[l]], axis=1)
    wr_hi = wr2.astype(BF16)
    wr_lo = (wr2 - wr_hi.astype(F32)).astype(BF16)
    br2 = row(jnp.concatenate([b_router[l], b_router[l]]).astype(F32))
    w1s_b, w3s_b, w2s_b = w1_s[l].astype(BF16), w3_s[l].astype(BF16), w2_s[l].astype(BF16)
    lru_args = (conv_w[l], row(conv_b[l]), wa_b, row(b_gate_a[l]), wx_b, row(b_gate_x[l]), row(lru_lambda[l]))

    def mix_and_route(yl, ya, x, mods, rows_per_mod, tm):
        return _out_proj_call(yl, ya, x, row(g_out_lru[l]), row(g_out_att[l]), w_out_b, mods[2], mods[4], mods[3],
                              mods[5], row(g_norm2[l]), wr_hi, wr_lo, br2, w1s_b, w3s_b, w2s_b, rows_per_mod, tm)

    xp = x_prompt.reshape(nb * s, d)
    proj_p = _in_proj_call(xp, row(g_norm1[l]), mods_p[1], mods_p[0], w_in_b, s, 512)
    yl_p, hl_p = _lru_prompt_call(proj_p, nb, *lru_args)
    ya_p = _attn_prompt_call(proj_p, nb, sb_bias[l])
    xs_p, h2_p, gate_p, rank_p, cum_p = mix_and_route(yl_p.reshape(nb * s, w), ya_p, xp, mods_p, s, 256)

    xsm = x_sample.reshape(db, d)
    proj_s = _in_proj_call(xsm, row(g_norm1[l]), mods_s[1], mods_s[0], w_in_b, 1, db)
    yl_s, hl_s = _lru_sample_call(proj_s, state_conv[l].reshape(db, (CONV_W - 1) * w), state_lru_h[l], *lru_args)
    ya_s = _attn_sample_call(proj_s[2], cache_k[l], cache_v[l], page_table, sb_bias[l])
    xs_s, h2_s, gate_s, rank_s, cum_s = mix_and_route(yl_s, ya_s, xsm, mods_s, 1, db)

    block_e, block_valid, block_idx, n_rows, pos_p, w8_p, pos_s, w8_s = _plan(
        rank_p, cum_p, gate_p, rank_s, cum_s, gate_s, n_exp)
    x_sorted = _scatter_call(h2_p, pos_p, n_rows)
    x_sorted = _scatter_call(h2_s, pos_s, n_rows, xs_prev=x_sorted, tb=db)
    y_sorted = _experts_call(block_e, block_valid, block_idx, x_sorted, w1_e[l], w3_e[l], w2_e[l])
    y_p = _combine_call(pos_p, w8_p, xs_p, mods_p[5], row(g_final), y_sorted, s)
    y_s = _combine_call(pos_s, w8_s, xs_s, mods_s[5], row(g_final), y_sorted, 1)

    nh = w // HEAD_DIM
    tail = CONV_W - 1
    k_p = proj_p[3].reshape(1, nb, s, nh, HEAD_DIM)
    v_p = proj_p[4].reshape(1, nb, s, nh, HEAD_DIM)
    conv_p = proj_p[0].reshape(nb, s, w)[:, s - tail:, :][None]
    k_s = proj_s[3].reshape(1, db, 1, nh, HEAD_DIM)
    v_s = proj_s[4].reshape(1, db, 1, nh, HEAD_DIM)
    conv_s = jnp.concatenate([state_conv[l][:, 1:, :], proj_s[0][:, None, :]], axis=1)[None]
    return (y_p.reshape(nb, s, d), y_s.reshape(db, 1, d), k_p, v_p, hl_p[None], conv_p,
            k_s, v_s, hl_s[None], conv_s)
```

```python
import functools

import jax
import jax.numpy as jnp
from jax import lax
from jax.experimental import pallas as pl
from jax.experimental.pallas import tpu as pltpu

F32 = jnp.float32
BF16 = jnp.bfloat16

EPS = 1e-6
LRU_C = 8.0
CONV_W = 4
LRU_BLOCKS = 8
N_GROUPS = 8
TOPK_GROUPS = 4
TOP_K = 8
ROUTED_SCALE = 2.5
LANES = 128
HEAD_DIM = 128
MOE_BM = 512
MOE_SUB = 256
VMEM_LIMIT = 56 * 1024 * 1024


def _cparams(sem):
    return pltpu.CompilerParams(dimension_semantics=sem, vmem_limit_bytes=VMEM_LIMIT)


def _dot(a, b):
    return jnp.dot(a, b, preferred_element_type=F32)


def _dot_nt(a, b):
    return lax.dot_general(a, b, (((1,), (1,)), ((), ())), preferred_element_type=F32)


def _split(x):
    hi = x.astype(BF16)
    lo = (x - hi.astype(F32)).astype(BF16)
    return hi, lo


def _softplus(z):
    return jnp.maximum(z, 0.0) + jnp.log(1.0 + jnp.exp(-jnp.abs(z)))


def _sigmoid(z):
    return 1.0 / (1.0 + jnp.exp(-z))


def _silu(z):
    return z * _sigmoid(z)


def _gelu_tanh(x):
    return 0.5 * x * (1.0 + jnp.tanh(0.7978845608028654 * (x + 0.044715 * (x * x * x))))


def _rms(x, g):
    return x * lax.rsqrt(jnp.mean(x * x, axis=-1, keepdims=True) + EPS) * g


def _pack_pairs(x):
    half = x.shape[1] // 2
    return _pack_halves(x[:, :half], x[:, half:])


def _pack_halves(lo, hi):
    lo = lax.bitcast_convert_type(lo.astype(BF16).astype(F32), jnp.uint32)
    hi = lax.bitcast_convert_type(hi.astype(BF16).astype(F32), jnp.uint32)
    return (lo >> 16) | hi


def _unpack_pairs(w):
    lo = lax.bitcast_convert_type(w << 16, F32)
    hi = lax.bitcast_convert_type(w & jnp.uint32(0xFFFF0000), F32)
    return lo, hi


def _mod_kernel(c_ref, w_ref, b_ref, o_ref):
    a = _silu(c_ref[...]).astype(BF16)
    o_ref[...] = _dot(a, w_ref[...].astype(BF16)) + b_ref[...]


def _mod_call(c_all, w_mod, b_mod, tn=512):
    m, d = c_all.shape
    n = w_mod.shape[1]
    return pl.pallas_call(
        _mod_kernel,
        grid=(n // tn,),
        in_specs=[pl.BlockSpec((m, d), lambda j: (0, 0)),
                  pl.BlockSpec((d, tn), lambda j: (0, j)),
                  pl.BlockSpec((1, tn), lambda j: (0, j))],
        out_specs=pl.BlockSpec((m, tn), lambda j: (0, j)),
        out_shape=jax.ShapeDtypeStruct((m, n), F32),
        compiler_params=_cparams(("arbitrary",)),
        name="mod",
    )(c_all, w_mod, b_mod)


def _mod_specs(mod, tm, d, rows_per_mod, n_grid_axes):
    if mod.ndim == 3:
        tiles = rows_per_mod // tm
        if n_grid_axes == 1:
            return pl.BlockSpec((None, 1, d), lambda i: (i // tiles, 0, 0))
        return pl.BlockSpec((None, 1, d), lambda i, j: (i // tiles, 0, 0))
    if n_grid_axes == 1:
        return pl.BlockSpec((tm, d), lambda i: (i, 0))
    return pl.BlockSpec((tm, d), lambda i, j: (i, 0))


def _in_proj_kernel(x_ref, g_ref, sc_ref, sh_ref, w_ref, o_ref, h_scr):
    @pl.when(pl.program_id(1) == 0)
    def _():
        h = _rms(x_ref[...], g_ref[...]) * (1.0 + sc_ref[...]) + sh_ref[...]
        h_scr[...] = h.astype(BF16)

    o_ref[...] = _dot(h_scr[...], w_ref[...])


def _in_proj_call(x, g, sc, sh, w_in, rows_per_mod, tm):
    t, d = x.shape
    w = w_in.shape[1] // 5
    return pl.pallas_call(
        _in_proj_kernel,
        grid=(t // tm, 5),
        in_specs=[pl.BlockSpec((tm, d), lambda i, j: (i, 0)),
                  pl.BlockSpec((1, d), lambda i, j: (0, 0)),
                  _mod_specs(sc, tm, d, rows_per_mod, 2),
                  _mod_specs(sh, tm, d, rows_per_mod, 2),
                  pl.BlockSpec((d, w), lambda i, j: (0, j))],
        out_specs=pl.BlockSpec((None, tm, w), lambda i, j: (j, i, 0)),
        out_shape=jax.ShapeDtypeStruct((5, t, w), F32),
        scratch_shapes=[pltpu.VMEM((tm, d), BF16)],
        compiler_params=_cparams(("arbitrary", "arbitrary")),
        name="in_proj",
    )(x, g, sc, sh, w_in)


def _lru_gates(xc, wa_ref, ba, wx_ref, bx, sp):
    xb = xc.astype(BF16)
    blk = xc.shape[1] // LRU_BLOCKS
    za = jnp.concatenate([_dot(xb[:, n * blk:(n + 1) * blk], wa_ref[n]) for n in range(LRU_BLOCKS)], axis=1)
    zx = jnp.concatenate([_dot(xb[:, n * blk:(n + 1) * blk], wx_ref[n]) for n in range(LRU_BLOCKS)], axis=1)
    r = _sigmoid(za + ba)
    i = _sigmoid(zx + bx)
    log_a = -LRU_C * r * sp
    a = jnp.exp(log_a)
    u = jnp.sqrt(1.0 - jnp.exp(2.0 * log_a)) * (i * xc)
    return a, u


def _lru_prompt_kernel(xl_ref, gt_ref, cw_ref, cb_ref, wa_ref, ba_ref, wx_ref, bx_ref, lam_ref,
                       y_ref, hl_ref, xp_scr, h_scr, a_scr, u_scr):
    nb, tc, w = xl_ref.shape
    pad = xp_scr.shape[1] - tc
    nc = w // LANES

    @pl.when(pl.program_id(0) == 0)
    def _():
        xp_scr[:, 0:pad, :] = jnp.zeros((nb, pad, w), F32)
        h_scr[...] = jnp.zeros_like(h_scr)

    sp = _softplus(-lam_ref[...])
    cw = cw_ref[...]
    for b in range(nb):
        x = xl_ref[b]
        xp_scr[b, pad:pad + tc, :] = x
        xc = cw[CONV_W - 1:CONV_W, :] * x + cb_ref[...]
        for j in range(CONV_W - 1):
            off = pad - (CONV_W - 1) + j
            xc = xc + cw[j:j + 1, :] * xp_scr[b, off:off + tc, :]
        a, u = _lru_gates(xc, wa_ref, ba_ref[...], wx_ref, bx_ref[...], sp)
        for c in range(nc):
            a_scr[c, b * tc:(b + 1) * tc, :] = a[:, c * LANES:(c + 1) * LANES]
            u_scr[c, b * tc:(b + 1) * tc, :] = u[:, c * LANES:(c + 1) * LANES]
        xp_scr[b, 0:pad, :] = x[tc - pad:tc, :]

    h = [h_scr[:, c * LANES:(c + 1) * LANES] for c in range(nc)]
    for t in range(tc):
        rows = pl.ds(t, nb, stride=tc)
        for c in range(nc):
            h[c] = a_scr[c, rows, :] * h[c] + u_scr[c, rows, :]
            u_scr[c, rows, :] = h[c]
    h = jnp.concatenate(h, axis=1)
    h_scr[...] = h
    hl_ref[...] = h

    for b in range(nb):
        hs = jnp.concatenate([u_scr[c, b * tc:(b + 1) * tc, :] for c in range(nc)], axis=1)
        y_ref[b] = (_gelu_tanh(gt_ref[b]) * hs).astype(BF16)


def _lru_prompt_call(proj5, nb, cw, cb, wa, ba, wx, bx, lam, tc=128):
    _, t, w = proj5.shape
    s = t // nb
    p4 = proj5.reshape(5, nb, s, w)
    full = lambda shape: pl.BlockSpec(shape, lambda c: (0,) * len(shape))
    return pl.pallas_call(
        _lru_prompt_kernel,
        grid=(s // tc,),
        in_specs=[pl.BlockSpec((None, nb, tc, w), lambda c: (0, 0, c, 0)),
                  pl.BlockSpec((None, nb, tc, w), lambda c: (1, 0, c, 0)),
                  full(cw.shape), full(cb.shape), full(wa.shape), full(ba.shape),
                  full(wx.shape), full(bx.shape), full(lam.shape)],
        out_specs=[pl.BlockSpec((nb, tc, w), lambda c: (0, c, 0)),
                   pl.BlockSpec((nb, w), lambda c: (0, 0))],
        out_shape=[jax.ShapeDtypeStruct((nb, s, w), BF16),
                   jax.ShapeDtypeStruct((nb, w), F32)],
        scratch_shapes=[pltpu.VMEM((nb, tc + 8, w), F32),
                        pltpu.VMEM((nb, w), F32),
                        pltpu.VMEM((w // LANES, nb * tc, LANES), F32),
                        pltpu.VMEM((w // LANES, nb * tc, LANES), F32)],
        compiler_params=_cparams(("arbitrary",)),
        name="lru_prompt",
    )(p4, p4, cw, cb, wa, ba, wx, bx, lam)


def _lru_sample_kernel(xl_ref, gt_ref, cs_ref, h0_ref, cw_ref, cb_ref, wa_ref, ba_ref, wx_ref, bx_ref,
                       lam_ref, y_ref, hl_ref):
    w = xl_ref.shape[1]
    x = xl_ref[...]
    cw = cw_ref[...]
    xc = cw[CONV_W - 1:CONV_W, :] * x + cb_ref[...]
    for j in range(CONV_W - 1):
        xc = xc + cw[j:j + 1, :] * cs_ref[:, j * w:(j + 1) * w]
    a, u = _lru_gates(xc, wa_ref, ba_ref[...], wx_ref, bx_ref[...], _softplus(-lam_ref[...]))
    h = a * h0_ref[...] + u
    hl_ref[...] = h
    y_ref[...] = (_gelu_tanh(gt_ref[...]) * h).astype(BF16)


def _lru_sample_call(proj5, conv_state, h0, cw, cb, wa, ba, wx, bx, lam):
    _, t, w = proj5.shape
    full = lambda shape: pl.BlockSpec(shape, lambda c: (0,) * len(shape))
    return pl.pallas_call(
        _lru_sample_kernel,
        grid=(1,),
        in_specs=[pl.BlockSpec((None, t, w), lambda c: (0, 0, 0)),
                  pl.BlockSpec((None, t, w), lambda c: (1, 0, 0)),
                  full(conv_state.shape), full(h0.shape),
                  full(cw.shape), full(cb.shape), full(wa.shape), full(ba.shape),
                  full(wx.shape), full(bx.shape), full(lam.shape)],
        out_specs=[full((t, w)), full((t, w))],
        out_shape=[jax.ShapeDtypeStruct((t, w), BF16), jax.ShapeDtypeStruct((t, w), F32)],
        compiler_params=_cparams(("arbitrary",)),
        name="lru_sample",
    )(proj5, proj5, conv_state, h0, cw, cb, wa, ba, wx, bx, lam)


def _strict_upper(n):
    j = lax.broadcasted_iota(jnp.int32, (n, n), 0)
    s = lax.broadcasted_iota(jnp.int32, (n, n), 1)
    return jnp.where(j > s, 1.0, 0.0).astype(BF16)


def _sb_block(q, k, v, bias, upper, r_run, acc, scale, causal):
    z = _dot_nt(q, k) * scale + bias
    sp = _softplus(z)
    lk = -sp
    if causal is not None:
        lk = jnp.where(causal, lk, 0.0)
    hi, lo = _split(lk)
    tq = q.shape[0]
    both = _dot(jnp.concatenate([hi, lo], axis=0), upper)
    later = both[:tq] + both[tq:]
    p = jnp.exp((z - sp) + later + r_run)
    if causal is not None:
        p = jnp.where(causal, p, 0.0)
    acc = acc + _dot(p.astype(BF16), v)
    r_run = r_run + jnp.sum(lk, axis=1, keepdims=True)
    return r_run, acc


def _attn_prompt_kernel(q_ref, k_ref, v_ref, b_ref, o_ref):
    tq = q_ref.shape[0]
    tk = b_ref.shape[-1]
    ratio = tq // tk
    hd = HEAD_DIM
    heads = q_ref.shape[1] // hd
    qi = pl.program_id(2)
    scale = hd ** -0.5
    upper = _strict_upper(tk)
    row = lax.broadcasted_iota(jnp.int32, (tq, tk), 0)
    col = lax.broadcasted_iota(jnp.int32, (tq, tk), 1)
    qs = [q_ref[:, h * hd:(h + 1) * hd].astype(BF16) for h in range(heads)]
    biases = [b_ref[h] for h in range(heads)]

    def blocks(kb, carry, causal):
        start = pl.multiple_of(kb * tk, tk)
        out = []
        for h in range(heads):
            k = k_ref[pl.ds(start, tk), h * hd:(h + 1) * hd].astype(BF16)
            v = v_ref[pl.ds(start, tk), h * hd:(h + 1) * hd].astype(BF16)
            out.append(_sb_block(qs[h], k, v, biases[h], upper, carry[h][0], carry[h][1], scale, causal))
        return tuple(out)

    carry = tuple((jnp.zeros((tq, 1), F32), jnp.zeros((tq, hd), F32)) for _ in range(heads))
    for r in reversed(range(ratio)):
        carry = blocks(qi * ratio + r, carry, col + r * tk < row)
    first = qi * ratio
    carry = lax.fori_loop(0, first, lambda n, c: blocks(first - 1 - n, c, None), carry)
    for h in range(heads):
        o_ref[:, h * hd:(h + 1) * hd] = carry[h][1].astype(BF16)


def _attn_prompt_call(proj5, nb, sb_bias, tq=512, tk=256, heads=2):
    _, t, w = proj5.shape
    s = t // nb
    nh = w // HEAD_DIM
    nq = s // tq
    hw = heads * HEAD_DIM
    bias = jnp.broadcast_to(sb_bias.astype(F32)[:, None, None], (nh, 1, tk))
    return pl.pallas_call(
        _attn_prompt_kernel,
        grid=(nb, nh // heads, nq),
        in_specs=[pl.BlockSpec((None, tq, hw), lambda b, h, i: (2, b * nq + i, h)),
                  pl.BlockSpec((None, s, hw), lambda b, h, i: (3, b, h)),
                  pl.BlockSpec((None, s, hw), lambda b, h, i: (4, b, h)),
                  pl.BlockSpec((heads, 1, tk), lambda b, h, i: (h, 0, 0))],
        out_specs=pl.BlockSpec((tq, hw), lambda b, h, i: (b * nq + i, h)),
        out_shape=jax.ShapeDtypeStruct((t, w), BF16),
        compiler_params=_cparams(("arbitrary", "arbitrary", "arbitrary")),
        name="attn_prompt",
    )(proj5, proj5, proj5, bias)


def _attn_sample_kernel(pt_ref, q_ref, b_ref, *refs, pages_per_step):
    k_refs = refs[:pages_per_step]
    v_refs = refs[pages_per_step:2 * pages_per_step]
    o_ref, r_scr, acc_scr = refs[2 * pages_per_step:]
    nh, hd = q_ref.shape
    page = k_refs[0].shape[0] // nh
    step = pl.program_id(1)

    @pl.when(step == 0)
    def _():
        r_scr[...] = jnp.zeros_like(r_scr)
        acc_scr[...] = jnp.zeros_like(acc_scr)

    head = lax.broadcasted_iota(jnp.int32, (nh, hd), 0)
    q = q_ref[...].astype(BF16)
    upper = _strict_upper(page)
    bias = b_ref[...]
    logits, within, totals = [], [], []
    for g in range(pages_per_step):
        z = jnp.zeros((nh, page), F32)
        for h in range(nh):
            kh = k_refs[g][pl.ds(h, page, stride=nh), :].astype(BF16)
            z = jnp.where(head == h, _dot_nt(q, kh), z)
        z = z * (hd ** -0.5) + bias
        sp = _softplus(z)
        hi, lo = _split(-sp)
        logits.append(z - sp)
        within.append(_dot(hi, upper) + _dot(lo, upper))
        totals.append(jnp.sum(-sp, axis=1, keepdims=True))
    r_run = r_scr[...]
    acc = acc_scr[...]
    for g in range(pages_per_step):
        p = jnp.exp(logits[g] + within[g] + r_run).astype(BF16)
        for h in range(nh):
            vh = v_refs[g][pl.ds(h, page, stride=nh), :].astype(BF16)
            acc = acc + jnp.where(head == h, _dot(p, vh), 0.0)
        r_run = r_run + totals[g]
    r_scr[...] = r_run
    acc_scr[...] = acc

    @pl.when(step == pl.num_programs(1) - 1)
    def _():
        o_ref[...] = acc.astype(BF16)


def _attn_sample_call(q, cache_k, cache_v, page_table, sb_bias, pages_per_step=8):
    db, w = q.shape
    n_pool, page, nh, hd = cache_k.shape
    n_pages = page_table.shape[1]
    ck = cache_k.reshape(n_pool, page * nh, hd)
    cv = cache_v.reshape(n_pool, page * nh, hd)
    bias = jnp.broadcast_to(sb_bias.astype(F32)[:, None], (nh, page))

    def page_spec(g):
        return pl.BlockSpec((None, page * nh, hd),
                            lambda b, s, pt: (pt[b, n_pages - 1 - (s * pages_per_step + g)], 0, 0))

    specs = [page_spec(g) for g in range(pages_per_step)]
    return pl.pallas_call(
        functools.partial(_attn_sample_kernel, pages_per_step=pages_per_step),
        grid_spec=pltpu.PrefetchScalarGridSpec(
            num_scalar_prefetch=1,
            grid=(db, n_pages // pages_per_step),
            in_specs=[pl.BlockSpec((None, nh, hd), lambda b, s, pt: (b, 0, 0)),
                      pl.BlockSpec((nh, page), lambda b, s, pt: (0, 0))] + specs + specs,
            out_specs=pl.BlockSpec((None, nh, hd), lambda b, s, pt: (b, 0, 0)),
            scratch_shapes=[pltpu.VMEM((nh, 1), F32), pltpu.VMEM((nh, hd), F32)]),
        out_shape=jax.ShapeDtypeStruct((db, nh, hd), BF16),
        compiler_params=_cparams(("arbitrary", "arbitrary")),
        name="attn_sample",
    )(page_table, q.reshape(db, nh, hd), bias, *([ck] * pages_per_step), *([cv] * pages_per_step)).reshape(db, w)


def _xor_partner(x, lane, s):
    n = x.shape[1]
    up = pltpu.roll(x, n - s, axis=1)
    dn = pltpu.roll(x, s, axis=1)
    return jnp.where((lane & s) == 0, up, dn)


def _route(h2, wr_hi_ref, wr_lo_ref, br):
    tm = h2.shape[0]
    n_exp = LANES // 2
    per_group = n_exp // N_GROUPS
    h_hi, h_lo = _split(h2)
    logits = _dot(h_hi, wr_hi_ref[...]) + _dot(h_lo, wr_hi_ref[...]) + _dot(h_hi, wr_lo_ref[...])
    scores = _sigmoid(logits)
    biased = scores + br
    lane = lax.broadcasted_iota(jnp.int32, (tm, LANES), 1)
    e = lane & (n_exp - 1)
    g = e >> (per_group.bit_length() - 1)
    m1 = biased
    m2 = jnp.full_like(biased, -jnp.inf)
    for s in (1, 2, 4):
        p1 = _xor_partner(m1, lane, s)
        p2 = _xor_partner(m2, lane, s)
        m1, m2 = jnp.maximum(m1, p1), jnp.maximum(jnp.minimum(m1, p1), jnp.maximum(m2, p2))
    gs = m1 + m2
    g_rank = jnp.zeros((tm, LANES), jnp.int32)
    for k in range(1, N_GROUPS):
        other = pltpu.roll(gs, per_group * k, axis=1)
        beats = (other > gs) | ((other == gs) & (g >= k))
        g_rank = g_rank + beats.astype(jnp.int32)
    masked = jnp.where(g_rank < TOPK_GROUPS, biased, -jnp.inf)
    e_rank = jnp.zeros((tm, LANES), jnp.int32)
    for k in range(1, n_exp):
        other = pltpu.roll(masked, k, axis=1)
        beats = (other > masked) | ((other == masked) & (e >= k))
        e_rank = e_rank + beats.astype(jnp.int32)
    first = lane < n_exp
    sel = (e_rank < TOP_K) & first
    wsel = jnp.where(sel, scores, 0.0)
    gate = wsel / jnp.sum(wsel, axis=1, keepdims=True) * ROUTED_SCALE
    return gate, jnp.where(first, e_rank, n_exp), sel


def _out_proj_kernel(yl_ref, ya_ref, x_ref, gl_ref, ga_ref, wl_ref, wa_ref, g1_ref, sc_ref, sh_ref, g2_ref,
                     gn_ref, wrh_ref, wrl_ref, br_ref, w1_ref, w3_ref, w2_ref,
                     xs_ref, h2_ref, gate_ref, rank_ref, cum_ref, cnt_scr):
    tm = x_ref.shape[0]

    @pl.when(pl.program_id(0) == 0)
    def _():
        cnt_scr[...] = jnp.zeros_like(cnt_scr)

    yl = _rms(yl_ref[...].astype(F32), gl_ref[...]).astype(BF16)
    ya = _rms(ya_ref[...].astype(F32), ga_ref[...]).astype(BF16)
    mixed = _dot(yl, wl_ref[...]) + _dot(ya, wa_ref[...])
    x1 = x_ref[...] + g1_ref[...] * mixed
    h2 = _rms(x1, gn_ref[...]) * (1.0 + sc_ref[...]) + sh_ref[...]
    h2_ref[...] = _pack_pairs(h2)

    gate, e_rank, sel = _route(h2, wrh_ref, wrl_ref, br_ref[...])
    gate_ref[...] = gate
    rank_ref[...] = e_rank
    i = lax.broadcasted_iota(jnp.int32, (tm, tm), 0)
    j = lax.broadcasted_iota(jnp.int32, (tm, tm), 1)
    lower = jnp.where(j < i, 1.0, 0.0).astype(BF16)
    selb = jnp.where(sel, 1.0, 0.0).astype(BF16)
    cum = _dot(lower, selb) + cnt_scr[...]
    cum_ref[...] = cum.astype(jnp.int32)
    cnt_scr[...] = cnt_scr[...] + jnp.sum(selb.astype(F32), axis=0, keepdims=True)

    hb = h2.astype(BF16)
    act = (_silu(_dot(hb, w1_ref[...])) * _dot(hb, w3_ref[...])).astype(BF16)
    shared = _dot(act, w2_ref[...])
    xs_ref[...] = x1 + g2_ref[...] * shared


def _out_proj_call(yl, ya, x, gl, ga, w_out, g1, sc2, sh2, g2, gn, wr_hi, wr_lo, br, w1s, w3s, w2s,
                   rows_per_mod, tm):
    t, d = x.shape
    w = yl.shape[1]
    full = lambda shape: pl.BlockSpec(shape, lambda i: (0,) * len(shape))
    mod = lambda m: _mod_specs(m, tm, d, rows_per_mod, 1)
    row = lambda n: pl.BlockSpec((tm, n), lambda i: (i, 0))
    return pl.pallas_call(
        _out_proj_kernel,
        grid=(t // tm,),
        in_specs=[row(w), row(w), row(d), full(gl.shape), full(ga.shape),
                  pl.BlockSpec((w, d), lambda i: (0, 0)), pl.BlockSpec((w, d), lambda i: (1, 0)),
                  mod(g1), mod(sc2), mod(sh2), mod(g2), full(gn.shape),
                  full(wr_hi.shape), full(wr_lo.shape), full(br.shape),
                  full(w1s.shape), full(w3s.shape), full(w2s.shape)],
        out_specs=[row(d), row(d // 2), row(LANES), row(LANES), row(LANES)],
        out_shape=[jax.ShapeDtypeStruct((t, d), F32), jax.ShapeDtypeStruct((t, d // 2), jnp.uint32),
                   jax.ShapeDtypeStruct((t, LANES), F32), jax.ShapeDtypeStruct((t, LANES), jnp.int32),
                   jax.ShapeDtypeStruct((t, LANES), jnp.int32)],
        scratch_shapes=[pltpu.VMEM((1, LANES), F32)],
        compiler_params=_cparams(("arbitrary",)),
        name="out_proj",
    )(yl, ya, x, gl, ga, w_out, w_out, g1, sc2, sh2, g2, gn, wr_hi, wr_lo, br, w1s, w3s, w2s)


def _row_copy(src, dst, src_row, dst_row, sem):
    return pltpu.make_async_copy(src.at[pl.ds(src_row, 1), :], dst.at[pl.ds(dst_row, 1), :], sem)


def _scatter_kernel(pos_ref, h_ref, *refs):
    xs_ref, sem = refs[-2], refs[-1]
    tb = h_ref.shape[0]

    def issue(t, c):
        for j in range(TOP_K):
            _row_copy(h_ref, xs_ref, t, pos_ref[t * TOP_K + j], sem).start(priority=j % 2)
        return c

    lax.fori_loop(0, tb, issue, 0)

    def drain(t, c):
        for j in range(TOP_K):
            _row_copy(h_ref, xs_ref, t, pos_ref[t * TOP_K + j], sem).wait()
        return c

    lax.fori_loop(0, tb, drain, 0)


def _scatter_call(h2, pos_flat, n_rows, xs_prev=None, tb=256):
    t, d = h2.shape
    any_spec = pl.BlockSpec(memory_space=pl.ANY)
    in_specs = [pl.BlockSpec((tb * TOP_K,), lambda i: (i,), memory_space=pltpu.SMEM),
                pl.BlockSpec((tb, d), lambda i: (i, 0))]
    args = [pos_flat, h2]
    aliases = {}
    if xs_prev is not None:
        in_specs.append(any_spec)
        args.append(xs_prev)
        aliases = {2: 0}
    return pl.pallas_call(
        _scatter_kernel,
        grid=(t // tb,),
        in_specs=in_specs,
        out_specs=any_spec,
        out_shape=jax.ShapeDtypeStruct((n_rows, d), h2.dtype),
        scratch_shapes=[pltpu.SemaphoreType.DMA(())],
        input_output_aliases=aliases,
        compiler_params=_cparams(("arbitrary",)),
        name="moe_scatter",
    )(*args)


def _experts_kernel(be_ref, bv_ref, bi_ref, x_ref, w1_ref, w3_ref, w2_ref, o_ref, w1_scr, w3_scr, w2_scr):
    i = pl.program_id(0)
    prev = be_ref[jnp.maximum(i - 1, 0)]

    @pl.when((i == 0) | (be_ref[i] != prev))
    def _():
        w1_scr[...] = w1_ref[...].astype(BF16)
        w3_scr[...] = w3_ref[...].astype(BF16)
        w2_scr[...] = w2_ref[...].astype(BF16)

    valid = bv_ref[i]
    bm, half = x_ref.shape
    n_sub = (valid + (MOE_SUB - 1)) >> (MOE_SUB.bit_length() - 1)

    def swiglu_rows(rows):
        row = lax.broadcasted_iota(jnp.int32, (rows, 1), 0)
        x_lo, x_hi = _unpack_pairs(jnp.where(row < valid, x_ref[0:rows, :], jnp.uint32(0)))
        x_lo, x_hi = x_lo.astype(BF16), x_hi.astype(BF16)
        h1 = _dot(x_lo, w1_scr[0:half, :]) + _dot(x_hi, w1_scr[half:2 * half, :])
        h3 = _dot(x_lo, w3_scr[0:half, :]) + _dot(x_hi, w3_scr[half:2 * half, :])
        act = (_silu(h1) * h3).astype(BF16)
        o_ref[0:rows, :] = _pack_halves(_dot(act, w2_scr[:, 0:half]), _dot(act, w2_scr[:, half:2 * half]))

    for k in range(1, bm // MOE_SUB + 1):
        pl.when(n_sub == k)(functools.partial(swiglu_rows, k * MOE_SUB))


def _experts_call(block_e, block_valid, block_idx, xs, w1, w3, w2):
    n_rows, d = xs.shape
    n_exp, _, ff = w1.shape
    d_full = w1.shape[1]
    bm = MOE_BM
    return pl.pallas_call(
        _experts_kernel,
        grid_spec=pltpu.PrefetchScalarGridSpec(
            num_scalar_prefetch=3,
            grid=(n_rows // bm,),
            in_specs=[pl.BlockSpec((bm, d), lambda i, be, bv, bi: (bi[i], 0)),
                      pl.BlockSpec((None, d_full, ff), lambda i, be, bv, bi: (be[i], 0, 0)),
                      pl.BlockSpec((None, d_full, ff), lambda i, be, bv, bi: (be[i], 0, 0)),
                      pl.BlockSpec((None, ff, d_full), lambda i, be, bv, bi: (be[i], 0, 0))],
            out_specs=pl.BlockSpec((bm, d), lambda i, be, bv, bi: (bi[i], 0)),
            scratch_shapes=[pltpu.VMEM((d_full, ff), BF16), pltpu.VMEM((d_full, ff), BF16),
                            pltpu.VMEM((ff, d_full), BF16)]),
        out_shape=jax.ShapeDtypeStruct((n_rows, d), jnp.uint32),
        compiler_params=_cparams(("arbitrary",)),
        name="moe_experts",
    )(block_e, block_valid, block_idx, xs, w1, w3, w2)


def _combine_kernel(pos_ref, w_ref, xs_ref, g2_ref, gf_ref, y_hbm, o_ref, buf, sem):
    tb = xs_ref.shape[0]

    def row_copy(t, j):
        return pltpu.make_async_copy(y_hbm.at[pl.ds(pos_ref[t * TOP_K + j], 1), :],
                                     buf.at[j, pl.ds(t, 1), :], sem)

    def issue(t, c):
        for j in range(TOP_K):
            row_copy(t, j).start(priority=j % 2)
        return c

    lax.fori_loop(0, tb, issue, 0)

    def drain(t, c):
        for j in range(TOP_K):
            row_copy(t, j).wait()
        return c

    lax.fori_loop(0, tb, drain, 0)

    wts = w_ref[...]
    r_lo = jnp.zeros(buf.shape[1:], F32)
    r_hi = jnp.zeros(buf.shape[1:], F32)
    for j in range(TOP_K):
        y_lo, y_hi = _unpack_pairs(buf[j])
        r_lo = r_lo + wts[:, j:j + 1] * y_lo
        r_hi = r_hi + wts[:, j:j + 1] * y_hi
    routed = jnp.concatenate([r_lo, r_hi], axis=1)
    x2 = xs_ref[...] + g2_ref[...] * routed
    o_ref[...] = _rms(x2, gf_ref[...])


def _combine_call(pos_flat, w8, xs, g2, gf, y_sorted, rows_per_mod, tb=128):
    t, d = xs.shape
    tb = min(tb, t)
    return pl.pallas_call(
        _combine_kernel,
        grid=(t // tb,),
        in_specs=[pl.BlockSpec((tb * TOP_K,), lambda i: (i,), memory_space=pltpu.SMEM),
                  pl.BlockSpec((tb, TOP_K), lambda i: (i, 0)),
                  pl.BlockSpec((tb, d), lambda i: (i, 0)),
                  _mod_specs(g2, tb, d, rows_per_mod, 1),
                  pl.BlockSpec((1, d), lambda i: (0, 0)),
                  pl.BlockSpec(memory_space=pl.ANY)],
        out_specs=pl.BlockSpec((tb, d), lambda i: (i, 0)),
        out_shape=jax.ShapeDtypeStruct((t, d), F32),
        scratch_shapes=[pltpu.VMEM((TOP_K, tb, d // 2), jnp.uint32), pltpu.SemaphoreType.DMA(())],
        compiler_params=_cparams(("arbitrary",)),
        name="moe_combine",
    )(pos_flat, w8, xs, g2, gf, y_sorted)


def _plan(e_rank_p, cum_p, gate_p, e_rank_s, cum_s, gate_s, n_exp):
    bm = MOE_BM
    sel_p = e_rank_p[:, :n_exp] < TOP_K
    sel_s = e_rank_s[:, :n_exp] < TOP_K
    cnt_p = jnp.sum(sel_p, axis=0, dtype=jnp.int32)
    counts = cnt_p + jnp.sum(sel_s, axis=0, dtype=jnp.int32)
    padded = (counts + bm - 1) // bm * bm
    pad_end = jnp.cumsum(padded)
    pad_start = pad_end - padded
    n_assign = (e_rank_p.shape[0] + e_rank_s.shape[0]) * TOP_K
    n_blocks = -(-n_assign // bm) + n_exp
    blk_start = jnp.arange(n_blocks, dtype=jnp.int32) * bm
    block_e = jnp.minimum(jnp.sum(pad_end[None, :] <= blk_start[:, None], axis=1, dtype=jnp.int32), n_exp - 1)
    block_valid = jnp.clip(pad_start[block_e] + counts[block_e] - blk_start, 0, bm).astype(jnp.int32)
    n_used = pad_end[-1] // bm
    block_idx = jnp.minimum(jnp.arange(n_blocks, dtype=jnp.int32), jnp.maximum(n_used - 1, 0)).astype(jnp.int32)
    slots = jnp.arange(TOP_K, dtype=jnp.int32)[None, :, None]

    def compact(e_rank, cum, gate, offset):
        dest = pad_start[None, :] + offset[None, :] + cum[:, :n_exp]
        hit = e_rank[:, None, :n_exp] == slots
        pos = jnp.sum(jnp.where(hit, dest[:, None, :], 0), axis=-1, dtype=jnp.int32)
        wts = jnp.sum(jnp.where(hit, gate[:, None, :n_exp], 0.0), axis=-1)
        return pos.reshape(-1), wts

    pos_p, w_p = compact(e_rank_p, cum_p, gate_p, jnp.zeros_like(cnt_p))
    pos_s, w_s = compact(e_rank_s, cum_s, gate_s, cnt_p)
    return block_e, block_valid, block_idx, n_blocks * bm, pos_p, w_p, pos_s, w_s


def kernel(x_prompt, x_sample, c_prompt, c_sample, cache_k, cache_v, page_table, state_lru_h, state_conv, w_mod, b_mod, g_norm1, g_norm2, w_in, conv_w, conv_b, w_gate_a, b_gate_a, w_gate_x, b_gate_x, lru_lambda, sb_bias, g_out_lru, g_out_att, w_out, w_router, b_router, w1_e, w3_e, w2_e, w1_s, w3_s, w2_s, g_final):
    nb, s, d = x_prompt.shape
    db = x_sample.shape[0]
    w = conv_w.shape[-1]
    n_exp = w_router.shape[-1]
    l = 0
    row = lambda v: v.reshape(1, -1)

    m = _mod_call(jnp.concatenate([c_prompt, c_sample], axis=0), w_mod[l], row(b_mod[l]))
    mods_p = [m[:nb, i * d:(i + 1) * d].reshape(nb, 1, d) for i in range(6)]
    mods_s = [m[nb:, i * d:(i + 1) * d] for i in range(6)]

    w_in_b = w_in[l].astype(BF16)
    w_out_b = w_out[l].astype(BF16)
    wa_b = w_gate_a[l].astype(BF16)
    wx_b = w_gate_x[l].astype(BF16)
    wr2 = jnp.concatenate([w_router[l], w_router[l]], axis=1)
    wr_hi = wr2.astype(BF16)
    wr_lo = (wr2 - wr_hi.astype(F32)).astype(BF16)
    br2 = row(jnp.concatenate([b_router[l], b_router[l]]).astype(F32))
    w1s_b, w3s_b, w2s_b = w1_s[l].astype(BF16), w3_s[l].astype(BF16), w2_s[l].astype(BF16)
    lru_args = (conv_w[l], row(conv_b[l]), wa_b, row(b_gate_a[l]), wx_b, row(b_gate_x[l]), row(lru_lambda[l]))

    def mix_and_route(yl, ya, x, mods, rows_per_mod, tm):
        return _out_proj_call(yl, ya, x, row(g_out_lru[l]), row(g_out_att[l]), w_out_b, mods[2], mods[4], mods[3],
                              mods[5], row(g_norm2[l]), wr_hi, wr_lo, br2, w1s_b, w3s_b, w2s_b, rows_per_mod, tm)

    xp = x_prompt.reshape(nb * s, d)
    proj_p = _in_proj_call(xp, row(g_norm1[l]), mods_p[1], mods_p[0], w_in_b, s, 512)
    yl_p, hl_p = _lru_prompt_call(proj_p, nb, *lru_args)
    ya_p = _attn_prompt_call(proj_p, nb, sb_bias[l])
    xs_p, h2_p, gate_p, rank_p, cum_p = mix_and_route(yl_p.reshape(nb * s, w), ya_p, xp, mods_p, s, 256)

    xsm = x_sample.reshape(db, d)
    proj_s = _in_proj_call(xsm, row(g_norm1[l]), mods_s[1], mods_s[0], w_in_b, 1, db)
    yl_s, hl_s = _lru_sample_call(proj_s, state_conv[l].reshape(db, (CONV_W - 1) * w), state_lru_h[l], *lru_args)
    ya_s = _attn_sample_call(proj_s[2], cache_k[l], cache_v[l], page_table, sb_bias[l])
    xs_s, h2_s, gate_s, rank_s, cum_s = mix_and_route(yl_s, ya_s, xsm, mods_s, 1, db)

    block_e, block_valid, block_idx, n_rows, pos_p, w8_p, pos_s, w8_s = _plan(
        rank_p, cum_p, gate_p, rank_s, cum_s, gate_s, n_exp)
    x_sorted = _scatter_call(h2_p, pos_p, n_rows)
    x_sorted = _scatter_call(h2_s, pos_s, n_rows, xs_prev=x_sorted, tb=db)
    y_sorted = _experts_call(block_e, block_valid, block_idx, x_sorted, w1_e[l], w3_e[l], w2_e[l])
    y_p = _combine_call(pos_p, w8_p, xs_p, mods_p[5], row(g_final), y_sorted, s)
    y_s = _combine_call(pos_s, w8_s, xs_s, mods_s[5], row(g_final), y_sorted, 1)

    nh = w // HEAD_DIM
    tail = CONV_W - 1
    k_p = proj_p[3].reshape(1, nb, s, nh, HEAD_DIM)
    v_p = proj_p[4].reshape(1, nb, s, nh, HEAD_DIM)
    conv_p = proj_p[0].reshape(nb, s, w)[:, s - tail:, :][None]
    k_s = proj_s[3].reshape(1, db, 1, nh, HEAD_DIM)
    v_s = proj_s[4].reshape(1, db, 1, nh, HEAD_DIM)
    conv_s = jnp.concatenate([state_conv[l][:, 1:, :], proj_s[0][:, None, :]], axis=1)[None]
    return (y_p.reshape(nb, s, d), y_s.reshape(db, 1, d), k_p, v_p, hl_p[None], conv_p,
            k_s, v_s, hl_s[None], conv_s)
```

```python
import functools

import jax
import jax.numpy as jnp
from jax import lax
from jax.experimental import pallas as pl
from jax.experimental.pallas import tpu as pltpu

F32 = jnp.float32
BF16 = jnp.bfloat16

EPS = 1e-6
LRU_C = 8.0
CONV_W = 4
LRU_BLOCKS = 8
N_GROUPS = 8
TOPK_GROUPS = 4
TOP_K = 8
ROUTED_SCALE = 2.5
LANES = 128
HEAD_DIM = 128
MOE_BM = 512
MOE_SUB = 256
VMEM_LIMIT = 56 * 1024 * 1024


def _cparams(sem):
    return pltpu.CompilerParams(dimension_semantics=sem, vmem_limit_bytes=VMEM_LIMIT)


def _dot(a, b):
    return jnp.dot(a, b, preferred_element_type=F32)


def _dot_nt(a, b):
    return lax.dot_general(a, b, (((1,), (1,)), ((), ())), preferred_element_type=F32)


def _split(x):
    hi = x.astype(BF16)
    lo = (x - hi.astype(F32)).astype(BF16)
    return hi, lo


def _softplus(z):
    return jnp.maximum(z, 0.0) + jnp.log(1.0 + jnp.exp(-jnp.abs(z)))


def _sigmoid(z):
    return 1.0 / (1.0 + jnp.exp(-z))


def _silu(z):
    return z * _sigmoid(z)


def _gelu_tanh(x):
    return 0.5 * x * (1.0 + jnp.tanh(0.7978845608028654 * (x + 0.044715 * (x * x * x))))


def _rms(x, g):
    return x * lax.rsqrt(jnp.mean(x * x, axis=-1, keepdims=True) + EPS) * g


def _pack_pairs(x):
    half = x.shape[1] // 2
    return _pack_halves(x[:, :half], x[:, half:])


def _pack_halves(lo, hi):
    lo = lax.bitcast_convert_type(lo.astype(BF16).astype(F32), jnp.uint32)
    hi = lax.bitcast_convert_type(hi.astype(BF16).astype(F32), jnp.uint32)
    return (lo >> 16) | hi


def _unpack_pairs(w):
    lo = lax.bitcast_convert_type(w << 16, F32)
    hi = lax.bitcast_convert_type(w & jnp.uint32(0xFFFF0000), F32)
    return lo, hi


def _mod_kernel(c_ref, w_ref, b_ref, o_ref):
    a = _silu(c_ref[...]).astype(BF16)
    o_ref[...] = _dot(a, w_ref[...].astype(BF16)) + b_ref[...]


def _mod_call(c_all, w_mod, b_mod, tn=512):
    m, d = c_all.shape
    n = w_mod.shape[1]
    return pl.pallas_call(
        _mod_kernel,
        grid=(n // tn,),
        in_specs=[pl.BlockSpec((m, d), lambda j: (0, 0)),
                  pl.BlockSpec((d, tn), lambda j: (0, j)),
                  pl.BlockSpec((1, tn), lambda j: (0, j))],
        out_specs=pl.BlockSpec((m, tn), lambda j: (0, j)),
        out_shape=jax.ShapeDtypeStruct((m, n), F32),
        compiler_params=_cparams(("arbitrary",)),
        name="mod",
    )(c_all, w_mod, b_mod)


def _mod_specs(mod, tm, d, rows_per_mod, n_grid_axes):
    if mod.ndim == 3:
        tiles = rows_per_mod // tm
        if n_grid_axes == 1:
            return pl.BlockSpec((None, 1, d), lambda i: (i // tiles, 0, 0))
        return pl.BlockSpec((None, 1, d), lambda i, j: (i // tiles, 0, 0))
    if n_grid_axes == 1:
        return pl.BlockSpec((tm, d), lambda i: (i, 0))
    return pl.BlockSpec((tm, d), lambda i, j: (i, 0))


def _in_proj_kernel(x_ref, g_ref, sc_ref, sh_ref, w_ref, o_ref, h_scr):
    @pl.when(pl.program_id(1) == 0)
    def _():
        h = _rms(x_ref[...], g_ref[...]) * (1.0 + sc_ref[...]) + sh_ref[...]
        h_scr[...] = h.astype(BF16)

    o_ref[...] = _dot(h_scr[...], w_ref[...])


def _in_proj_call(x, g, sc, sh, w_in, rows_per_mod, tm):
    t, d = x.shape
    w = w_in.shape[1] // 5
    return pl.pallas_call(
        _in_proj_kernel,
        grid=(t // tm, 5),
        in_specs=[pl.BlockSpec((tm, d), lambda i, j: (i, 0)),
                  pl.BlockSpec((1, d), lambda i, j: (0, 0)),
                  _mod_specs(sc, tm, d, rows_per_mod, 2),
                  _mod_specs(sh, tm, d, rows_per_mod, 2),
                  pl.BlockSpec((d, w), lambda i, j: (0, j))],
        out_specs=pl.BlockSpec((None, tm, w), lambda i, j: (j, i, 0)),
        out_shape=jax.ShapeDtypeStruct((5, t, w), F32),
        scratch_shapes=[pltpu.VMEM((tm, d), BF16)],
        compiler_params=_cparams(("arbitrary", "arbitrary")),
        name="in_proj",
    )(x, g, sc, sh, w_in)


def _lru_gates(xc, wa_ref, ba, wx_ref, bx, sp):
    xb = xc.astype(BF16)
    blk = xc.shape[1] // LRU_BLOCKS
    za = jnp.concatenate([_dot(xb[:, n * blk:(n + 1) * blk], wa_ref[n]) for n in range(LRU_BLOCKS)], axis=1)
    zx = jnp.concatenate([_dot(xb[:, n * blk:(n + 1) * blk], wx_ref[n]) for n in range(LRU_BLOCKS)], axis=1)
    r = _sigmoid(za + ba)
    i = _sigmoid(zx + bx)
    log_a = -LRU_C * r * sp
    a = jnp.exp(log_a)
    u = jnp.sqrt(1.0 - jnp.exp(2.0 * log_a)) * (i * xc)
    return a, u


def _lru_prompt_kernel(xl_ref, gt_ref, cw_ref, cb_ref, wa_ref, ba_ref, wx_ref, bx_ref, lam_ref,
                       y_ref, hl_ref, xp_scr, h_scr, a_scr, u_scr):
    nb, tc, w = xl_ref.shape
    pad = xp_scr.shape[1] - tc
    nc = w // LANES

    @pl.when(pl.program_id(0) == 0)
    def _():
        xp_scr[:, 0:pad, :] = jnp.zeros((nb, pad, w), F32)
        h_scr[...] = jnp.zeros_like(h_scr)

    sp = _softplus(-lam_ref[...])
    cw = cw_ref[...]
    for b in range(nb):
        x = xl_ref[b]
        xp_scr[b, pad:pad + tc, :] = x
        xc = cw[CONV_W - 1:CONV_W, :] * x + cb_ref[...]
        for j in range(CONV_W - 1):
            off = pad - (CONV_W - 1) + j
            xc = xc + cw[j:j + 1, :] * xp_scr[b, off:off + tc, :]
        a, u = _lru_gates(xc, wa_ref, ba_ref[...], wx_ref, bx_ref[...], sp)
        for c in range(nc):
            a_scr[c, b * tc:(b + 1) * tc, :] = a[:, c * LANES:(c + 1) * LANES]
            u_scr[c, b * tc:(b + 1) * tc, :] = u[:, c * LANES:(c + 1) * LANES]
        xp_scr[b, 0:pad, :] = x[tc - pad:tc, :]

    h = [h_scr[:, c * LANES:(c + 1) * LANES] for c in range(nc)]
    for t in range(tc):
        rows = pl.ds(t, nb, stride=tc)
        for c in range(nc):
            h[c] = a_scr[c, rows, :] * h[c] + u_scr[c, rows, :]
            u_scr[c, rows, :] = h[c]
    h = jnp.concatenate(h, axis=1)
    h_scr[...] = h
    hl_ref[...] = h

    for b in range(nb):
        hs = jnp.concatenate([u_scr[c, b * tc:(b + 1) * tc, :] for c in range(nc)], axis=1)
        y_ref[b] = (_gelu_tanh(gt_ref[b]) * hs).astype(BF16)


def _lru_prompt_call(proj5, nb, cw, cb, wa, ba, wx, bx, lam, tc=128):
    _, t, w = proj5.shape
    s = t // nb
    p4 = proj5.reshape(5, nb, s, w)
    full = lambda shape: pl.BlockSpec(shape, lambda c: (0,) * len(shape))
    return pl.pallas_call(
        _lru_prompt_kernel,
        grid=(s // tc,),
        in_specs=[pl.BlockSpec((None, nb, tc, w), lambda c: (0, 0, c, 0)),
                  pl.BlockSpec((None, nb, tc, w), lambda c: (1, 0, c, 0)),
                  full(cw.shape), full(cb.shape), full(wa.shape), full(ba.shape),
                  full(wx.shape), full(bx.shape), full(lam.shape)],
        out_specs=[pl.BlockSpec((nb, tc, w), lambda c: (0, c, 0)),
                   pl.BlockSpec((nb, w), lambda c: (0, 0))],
        out_shape=[jax.ShapeDtypeStruct((nb, s, w), BF16),
                   jax.ShapeDtypeStruct((nb, w), F32)],
        scratch_shapes=[pltpu.VMEM((nb, tc + 8, w), F32),
                        pltpu.VMEM((nb, w), F32),
                        pltpu.VMEM((w // LANES, nb * tc, LANES), F32),
                        pltpu.VMEM((w // LANES, nb * tc, LANES), F32)],
        compiler_params=_cparams(("arbitrary",)),
        name="lru_prompt",
    )(p4, p4, cw, cb, wa, ba, wx, bx, lam)


def _lru_sample_kernel(xl_ref, gt_ref, cs_ref, h0_ref, cw_ref, cb_ref, wa_ref, ba_ref, wx_ref, bx_ref,
                       lam_ref, y_ref, hl_ref):
    w = xl_ref.shape[1]
    x = xl_ref[...]
    cw = cw_ref[...]
    xc = cw[CONV_W - 1:CONV_W, :] * x + cb_ref[...]
    for j in range(CONV_W - 1):
        xc = xc + cw[j:j + 1, :] * cs_ref[:, j * w:(j + 1) * w]
    a, u = _lru_gates(xc, wa_ref, ba_ref[...], wx_ref, bx_ref[...], _softplus(-lam_ref[...]))
    h = a * h0_ref[...] + u
    hl_ref[...] = h
    y_ref[...] = (_gelu_tanh(gt_ref[...]) * h).astype(BF16)


def _lru_sample_call(proj5, conv_state, h0, cw, cb, wa, ba, wx, bx, lam):
    _, t, w = proj5.shape
    full = lambda shape: pl.BlockSpec(shape, lambda c: (0,) * len(shape))
    return pl.pallas_call(
        _lru_sample_kernel,
        grid=(1,),
        in_specs=[pl.BlockSpec((None, t, w), lambda c: (0, 0, 0)),
                  pl.BlockSpec((None, t, w), lambda c: (1, 0, 0)),
                  full(conv_state.shape), full(h0.shape),
                  full(cw.shape), full(cb.shape), full(wa.shape), full(ba.shape),
                  full(wx.shape), full(bx.shape), full(lam.shape)],
        out_specs=[full((t, w)), full((t, w))],
        out_shape=[jax.ShapeDtypeStruct((t, w), BF16), jax.ShapeDtypeStruct((t, w), F32)],
        compiler_params=_cparams(("arbitrary",)),
        name="lru_sample",
    )(proj5, proj5, conv_state, h0, cw, cb, wa, ba, wx, bx, lam)


def _strict_upper(n):
    j = lax.broadcasted_iota(jnp.int32, (n, n), 0)
    s = lax.broadcasted_iota(jnp.int32, (n, n), 1)
    return jnp.where(j > s, 1.0, 0.0).astype(BF16)


def _sb_block(q, k, v, bias, upper, r_run, acc, scale, causal):
    z = _dot_nt(q, k) * scale + bias
    sp = _softplus(z)
    lk = -sp
    if causal is not None:
        lk = jnp.where(causal, lk, 0.0)
    hi, lo = _split(lk)
    tq = q.shape[0]
    both = _dot(jnp.concatenate([hi, lo], axis=0), upper)
    later = both[:tq] + both[tq:]
    p = jnp.exp((z - sp) + later + r_run)
    if causal is not None:
        p = jnp.where(causal, p, 0.0)
    acc = acc + _dot(p.astype(BF16), v)
    r_run = r_run + jnp.sum(lk, axis=1, keepdims=True)
    return r_run, acc


def _attn_prompt_kernel(q_ref, k_ref, v_ref, b_ref, o_ref):
    tq = q_ref.shape[0]
    tk = b_ref.shape[-1]
    ratio = tq // tk
    hd = HEAD_DIM
    heads = q_ref.shape[1] // hd
    qi = pl.program_id(2)
    scale = hd ** -0.5
    upper = _strict_upper(tk)
    row = lax.broadcasted_iota(jnp.int32, (tq, tk), 0)
    col = lax.broadcasted_iota(jnp.int32, (tq, tk), 1)
    qs = [q_ref[:, h * hd:(h + 1) * hd].astype(BF16) for h in range(heads)]
    biases = [b_ref[h] for h in range(heads)]

    def blocks(kb, carry, causal):
        start = pl.multiple_of(kb * tk, tk)
        out = []
        for h in range(heads):
            k = k_ref[pl.ds(start, tk), h * hd:(h + 1) * hd].astype(BF16)
            v = v_ref[pl.ds(start, tk), h * hd:(h + 1) * hd].astype(BF16)
            out.append(_sb_block(qs[h], k, v, biases[h], upper, carry[h][0], carry[h][1], scale, causal))
        return tuple(out)

    carry = tuple((jnp.zeros((tq, 1), F32), jnp.zeros((tq, hd), F32)) for _ in range(heads))
    for r in reversed(range(ratio)):
        carry = blocks(qi * ratio + r, carry, col + r * tk < row)
    first = qi * ratio
    carry = lax.fori_loop(0, first, lambda n, c: blocks(first - 1 - n, c, None), carry)
    for h in range(heads):
        o_ref[:, h * hd:(h + 1) * hd] = carry[h][1].astype(BF16)


def _attn_prompt_call(proj5, nb, sb_bias, tq=512, tk=256, heads=2):
    _, t, w = proj5.shape
    s = t // nb
    nh = w // HEAD_DIM
    nq = s // tq
    hw = heads * HEAD_DIM
    bias = jnp.broadcast_to(sb_bias.astype(F32)[:, None, None], (nh, 1, tk))
    return pl.pallas_call(
        _attn_prompt_kernel,
        grid=(nb, nh // heads, nq),
        in_specs=[pl.BlockSpec((None, tq, hw), lambda b, h, i: (2, b * nq + i, h)),
                  pl.BlockSpec((None, s, hw), lambda b, h, i: (3, b, h)),
                  pl.BlockSpec((None, s, hw), lambda b, h, i: (4, b, h)),
                  pl.BlockSpec((heads, 1, tk), lambda b, h, i: (h, 0, 0))],
        out_specs=pl.BlockSpec((tq, hw), lambda b, h, i: (b * nq + i, h)),
        out_shape=jax.ShapeDtypeStruct((t, w), BF16),
        compiler_params=_cparams(("arbitrary", "arbitrary", "arbitrary")),
        name="attn_prompt",
    )(proj5, proj5, proj5, bias)


def _attn_sample_kernel(pt_ref, q_ref, b_ref, *refs, pages_per_step):
    k_refs = refs[:pages_per_step]
    v_refs = refs[pages_per_step:2 * pages_per_step]
    o_ref, r_scr, acc_scr = refs[2 * pages_per_step:]
    nh, hd = q_ref.shape
    page = k_refs[0].shape[0] // nh
    step = pl.program_id(1)

    @pl.when(step == 0)
    def _():
        r_scr[...] = jnp.zeros_like(r_scr)
        acc_scr[...] = jnp.zeros_like(acc_scr)

    head = lax.broadcasted_iota(jnp.int32, (nh, hd), 0)
    q = q_ref[...].astype(BF16)
    upper = _strict_upper(page)
    bias = b_ref[...]
    logits, within, totals = [], [], []
    for g in range(pages_per_step):
        z = jnp.zeros((nh, page), F32)
        for h in range(nh):
            kh = k_refs[g][pl.ds(h, page, stride=nh), :].astype(BF16)
            z = jnp.where(head == h, _dot_nt(q, kh), z)
        z = z * (hd ** -0.5) + bias
        sp = _softplus(z)
        hi, lo = _split(-sp)
        logits.append(z - sp)
        within.append(_dot(hi, upper) + _dot(lo, upper))
        totals.append(jnp.sum(-sp, axis=1, keepdims=True))
    r_run = r_scr[...]
    acc = acc_scr[...]
    for g in range(pages_per_step):
        p = jnp.exp(logits[g] + within[g] + r_run).astype(BF16)
        for h in range(nh):
            vh = v_refs[g][pl.ds(h, page, stride=nh), :].astype(BF16)
            acc = acc + jnp.where(head == h, _dot(p, vh), 0.0)
        r_run = r_run + totals[g]
    r_scr[...] = r_run
    acc_scr[...] = acc

    @pl.when(step == pl.num_programs(1) - 1)
    def _():
        o_ref[...] = acc.astype(BF16)


def _attn_sample_call(q, cache_k, cache_v, page_table, sb_bias, pages_per_step=8):
    db, w = q.shape
    n_pool, page, nh, hd = cache_k.shape
    n_pages = page_table.shape[1]
    ck = cache_k.reshape(n_pool, page * nh, hd)
    cv = cache_v.reshape(n_pool, page * nh, hd)
    bias = jnp.broadcast_to(sb_bias.astype(F32)[:, None], (nh, page))

    def page_spec(g):
        return pl.BlockSpec((None, page * nh, hd),
                            lambda b, s, pt: (pt[b, n_pages - 1 - (s * pages_per_step + g)], 0, 0))

    specs = [page_spec(g) for g in range(pages_per_step)]
    return pl.pallas_call(
        functools.partial(_attn_sample_kernel, pages_per_step=pages_per_step),
        grid_spec=pltpu.PrefetchScalarGridSpec(
            num_scalar_prefetch=1,
            grid=(db, n_pages // pages_per_step),
            in_specs=[pl.BlockSpec((None, nh, hd), lambda b, s, pt: (b, 0, 0)),
                      pl.BlockSpec((nh, page), lambda b, s, pt: (0, 0))] + specs + specs,
            out_specs=pl.BlockSpec((None, nh, hd), lambda b, s, pt: (b, 0, 0)),
            scratch_shapes=[pltpu.VMEM((nh, 1), F32), pltpu.VMEM((nh, hd), F32)]),
        out_shape=jax.ShapeDtypeStruct((db, nh, hd), BF16),
        compiler_params=_cparams(("arbitrary", "arbitrary")),
        name="attn_sample",
    )(page_table, q.reshape(db, nh, hd), bias, *([ck] * pages_per_step), *([cv] * pages_per_step)).reshape(db, w)


def _xor_partner(x, lane, s):
    n = x.shape[1]
    up = pltpu.roll(x, n - s, axis=1)
    dn = pltpu.roll(x, s, axis=1)
    return jnp.where((lane & s) == 0, up, dn)


def _route(h2, wr_hi_ref, wr_lo_ref, br):
    tm = h2.shape[0]
    n_exp = LANES // 2
    per_group = n_exp // N_GROUPS
    h_hi, h_lo = _split(h2)
    logits = _dot(h_hi, wr_hi_ref[...]) + _dot(h_lo, wr_hi_ref[...]) + _dot(h_hi, wr_lo_ref[...])
    scores = _sigmoid(logits)
    biased = scores + br
    lane = lax.broadcasted_iota(jnp.int32, (tm, LANES), 1)
    e = lane & (n_exp - 1)
    g = e >> (per_group.bit_length() - 1)
    m1 = biased
    m2 = jnp.full_like(biased, -jnp.inf)
    for s in (1, 2, 4):
        p1 = _xor_partner(m1, lane, s)
        p2 = _xor_partner(m2, lane, s)
        m1, m2 = jnp.maximum(m1, p1), jnp.maximum(jnp.minimum(m1, p1), jnp.maximum(m2, p2))
    gs = m1 + m2
    g_rank = jnp.zeros((tm, LANES), jnp.int32)
    for k in range(1, N_GROUPS):
        other = pltpu.roll(gs, per_group * k, axis=1)
        beats = (other > gs) | ((other == gs) & (g >= k))
        g_rank = g_rank + beats.astype(jnp.int32)
    masked = jnp.where(g_rank < TOPK_GROUPS, biased, -jnp.inf)
    e_rank = jnp.zeros((tm, LANES), jnp.int32)
    for k in range(1, n_exp):
        other = pltpu.roll(masked, k, axis=1)
        beats = (other > masked) | ((other == masked) & (e >= k))
        e_rank = e_rank + beats.astype(jnp.int32)
    first = lane < n_exp
    sel = (e_rank < TOP_K) & first
    wsel = jnp.where(sel, scores, 0.0)
    gate = wsel / jnp.sum(wsel, axis=1, keepdims=True) * ROUTED_SCALE
    return gate, jnp.where(first, e_rank, n_exp), sel


def _out_proj_kernel(yl_ref, ya_ref, x_ref, gl_ref, ga_ref, wl_ref, wa_ref, g1_ref, sc_ref, sh_ref, g2_ref,
                     gn_ref, wrh_ref, wrl_ref, br_ref, w1_ref, w3_ref, w2_ref,
                     xs_ref, h2_ref, gate_ref, rank_ref, cum_ref, cnt_scr):
    tm = x_ref.shape[0]

    @pl.when(pl.program_id(0) == 0)
    def _():
        cnt_scr[...] = jnp.zeros_like(cnt_scr)

    yl = _rms(yl_ref[...].astype(F32), gl_ref[...]).astype(BF16)
    ya = _rms(ya_ref[...].astype(F32), ga_ref[...]).astype(BF16)
    mixed = _dot(yl, wl_ref[...]) + _dot(ya, wa_ref[...])
    x1 = x_ref[...] + g1_ref[...] * mixed
    h2 = _rms(x1, gn_ref[...]) * (1.0 + sc_ref[...]) + sh_ref[...]
    h2_ref[...] = _pack_pairs(h2)

    gate, e_rank, sel = _route(h2, wrh_ref, wrl_ref, br_ref[...])
    gate_ref[...] = gate
    rank_ref[...] = e_rank
    i = lax.broadcasted_iota(jnp.int32, (tm, tm), 0)
    j = lax.broadcasted_iota(jnp.int32, (tm, tm), 1)
    lower = jnp.where(j < i, 1.0, 0.0).astype(BF16)
    selb = jnp.where(sel, 1.0, 0.0).astype(BF16)
    cum = _dot(lower, selb) + cnt_scr[...]
    cum_ref[...] = cum.astype(jnp.int32)
    cnt_scr[...] = cnt_scr[...] + jnp.sum(selb.astype(F32), axis=0, keepdims=True)

    hb = h2.astype(BF16)
    act = (_silu(_dot(hb, w1_ref[...])) * _dot(hb, w3_ref[...])).astype(BF16)
    shared = _dot(act, w2_ref[...])
    xs_ref[...] = x1 + g2_ref[...] * shared


def _out_proj_call(yl, ya, x, gl, ga, w_out, g1, sc2, sh2, g2, gn, wr_hi, wr_lo, br, w1s, w3s, w2s,
                   rows_per_mod, tm):
    t, d = x.shape
    w = yl.shape[1]
    full = lambda shape: pl.BlockSpec(shape, lambda i: (0,) * len(shape))
    mod = lambda m: _mod_specs(m, tm, d, rows_per_mod, 1)
    row = lambda n: pl.BlockSpec((tm, n), lambda i: (i, 0))
    return pl.pallas_call(
        _out_proj_kernel,
        grid=(t // tm,),
        in_specs=[row(w), row(w), row(d), full(gl.shape), full(ga.shape),
                  pl.BlockSpec((w, d), lambda i: (0, 0)), pl.BlockSpec((w, d), lambda i: (1, 0)),
                  mod(g1), mod(sc2), mod(sh2), mod(g2), full(gn.shape),
                  full(wr_hi.shape), full(wr_lo.shape), full(br.shape),
                  full(w1s.shape), full(w3s.shape), full(w2s.shape)],
        out_specs=[row(d), row(d // 2), row(LANES), row(LANES), row(LANES)],
        out_shape=[jax.ShapeDtypeStruct((t, d), F32), jax.ShapeDtypeStruct((t, d // 2), jnp.uint32),
                   jax.ShapeDtypeStruct((t, LANES), F32), jax.ShapeDtypeStruct((t, LANES), jnp.int32),
                   jax.ShapeDtypeStruct((t, LANES), jnp.int32)],
        scratch_shapes=[pltpu.VMEM((1, LANES), F32)],
        compiler_params=_cparams(("arbitrary",)),
        name="out_proj",
    )(yl, ya, x, gl, ga, w_out, w_out, g1, sc2, sh2, g2, gn, wr_hi, wr_lo, br, w1s, w3s, w2s)


def _row_copy(src, dst, src_row, dst_row, sem):
    return pltpu.make_async_copy(src.at[pl.ds(src_row, 1), :], dst.at[pl.ds(dst_row, 1), :], sem)


def _scatter_kernel(pos_ref, h_ref, *refs):
    xs_ref, sem = refs[-2], refs[-1]
    tb = h_ref.shape[0]

    def issue(t, c):
        for j in range(TOP_K):
            _row_copy(h_ref, xs_ref, t, pos_ref[t * TOP_K + j], sem).start(priority=j % 2)
        return c

    lax.fori_loop(0, tb, issue, 0)

    def drain(t, c):
        for j in range(TOP_K):
            _row_copy(h_ref, xs_ref, t, pos_ref[t * TOP_K + j], sem).wait()
        return c

    lax.fori_loop(0, tb, drain, 0)


def _scatter_call(h2, pos_flat, n_rows, xs_prev=None, tb=256):
    t, d = h2.shape
    any_spec = pl.BlockSpec(memory_space=pl.ANY)
    in_specs = [pl.BlockSpec((tb * TOP_K,), lambda i: (i,), memory_space=pltpu.SMEM),
                pl.BlockSpec((tb, d), lambda i: (i, 0))]
    args = [pos_flat, h2]
    aliases = {}
    if xs_prev is not None:
        in_specs.append(any_spec)
        args.append(xs_prev)
        aliases = {2: 0}
    return pl.pallas_call(
        _scatter_kernel,
        grid=(t // tb,),
        in_specs=in_specs,
        out_specs=any_spec,
        out_shape=jax.ShapeDtypeStruct((n_rows, d), h2.dtype),
        scratch_shapes=[pltpu.SemaphoreType.DMA(())],
        input_output_aliases=aliases,
        compiler_params=_cparams(("arbitrary",)),
        name="moe_scatter",
    )(*args)


def _experts_kernel(be_ref, bv_ref, bi_ref, x_ref, w1_ref, w3_ref, w2_ref, o_ref, w1_scr, w3_scr, w2_scr):
    i = pl.program_id(0)
    prev = be_ref[jnp.maximum(i - 1, 0)]

    @pl.when((i == 0) | (be_ref[i] != prev))
    def _():
        w1_scr[...] = w1_ref[...].astype(BF16)
        w3_scr[...] = w3_ref[...].astype(BF16)
        w2_scr[...] = w2_ref[...].astype(BF16)

    valid = bv_ref[i]
    bm, half = x_ref.shape
    n_sub = (valid + (MOE_SUB - 1)) >> (MOE_SUB.bit_length() - 1)

    def swiglu_rows(rows):
        row = lax.broadcasted_iota(jnp.int32, (rows, 1), 0)
        x_lo, x_hi = _unpack_pairs(jnp.where(row < valid, x_ref[0:rows, :], jnp.uint32(0)))
        x_lo, x_hi = x_lo.astype(BF16), x_hi.astype(BF16)
        h1 = _dot(x_lo, w1_scr[0:half, :]) + _dot(x_hi, w1_scr[half:2 * half, :])
        h3 = _dot(x_lo, w3_scr[0:half, :]) + _dot(x_hi, w3_scr[half:2 * half, :])
        act = (_silu(h1) * h3).astype(BF16)
        o_ref[0:rows, :] = _pack_halves(_dot(act, w2_scr[:, 0:half]), _dot(act, w2_scr[:, half:2 * half]))

    for k in range(1, bm // MOE_SUB + 1):
        pl.when(n_sub == k)(functools.partial(swiglu_rows, k * MOE_SUB))


def _experts_call(block_e, block_valid, block_idx, xs, w1, w3, w2):
    n_rows, d = xs.shape
    n_exp, _, ff = w1.shape
    d_full = w1.shape[1]
    bm = MOE_BM
    return pl.pallas_call(
        _experts_kernel,
        grid_spec=pltpu.PrefetchScalarGridSpec(
            num_scalar_prefetch=3,
            grid=(n_rows // bm,),
            in_specs=[pl.BlockSpec((bm, d), lambda i, be, bv, bi: (bi[i], 0)),
                      pl.BlockSpec((None, d_full, ff), lambda i, be, bv, bi: (be[i], 0, 0)),
                      pl.BlockSpec((None, d_full, ff), lambda i, be, bv, bi: (be[i], 0, 0)),
                      pl.BlockSpec((None, ff, d_full), lambda i, be, bv, bi: (be[i], 0, 0))],
            out_specs=pl.BlockSpec((bm, d), lambda i, be, bv, bi: (bi[i], 0)),
            scratch_shapes=[pltpu.VMEM((d_full, ff), BF16), pltpu.VMEM((d_full, ff), BF16),
                            pltpu.VMEM((ff, d_full), BF16)]),
        out_shape=jax.ShapeDtypeStruct((n_rows, d), jnp.uint32),
        compiler_params=_cparams(("arbitrary",)),
        name="moe_experts",
    )(block_e, block_valid, block_idx, xs, w1, w3, w2)


def _combine_kernel(pos_ref, pos_next_ref, w_ref, xs_ref, g2_ref, gf_ref, y_hbm, o_ref, buf, sem):
    tb = xs_ref.shape[0]
    i = pl.program_id(0)
    slot = i % 2

    def row_copy(p_ref, s, t, j):
        return pltpu.make_async_copy(y_hbm.at[pl.ds(p_ref[t * TOP_K + j], 1), :],
                                     buf.at[s, j, pl.ds(t, 1), :], sem.at[s])

    def issue(p_ref, s):
        def body(t, c):
            for j in range(TOP_K):
                row_copy(p_ref, s, t, j).start(priority=j % 2)
            return c
        lax.fori_loop(0, tb, body, 0)

    @pl.when(i == 0)
    def _():
        issue(pos_ref, 0)

    @pl.when(i + 1 < pl.num_programs(0))
    def _():
        issue(pos_next_ref, 1 - slot)

    def drain(t, c):
        for j in range(TOP_K):
            row_copy(pos_ref, slot, t, j).wait()
        return c

    lax.fori_loop(0, tb, drain, 0)

    wts = w_ref[...]
    r_lo = jnp.zeros(buf.shape[2:], F32)
    r_hi = jnp.zeros(buf.shape[2:], F32)
    for j in range(TOP_K):
        y_lo, y_hi = _unpack_pairs(buf[slot, j])
        r_lo = r_lo + wts[:, j:j + 1] * y_lo
        r_hi = r_hi + wts[:, j:j + 1] * y_hi
    routed = jnp.concatenate([r_lo, r_hi], axis=1)
    x2 = xs_ref[...] + g2_ref[...] * routed
    o_ref[...] = _rms(x2, gf_ref[...])


def _combine_call(pos_flat, w8, xs, g2, gf, y_sorted, rows_per_mod, tb=128):
    t, d = xs.shape
    tb = min(tb, t)
    steps = t // tb
    return pl.pallas_call(
        _combine_kernel,
        grid=(steps,),
        in_specs=[pl.BlockSpec((tb * TOP_K,), lambda i: (i,), memory_space=pltpu.SMEM),
                  pl.BlockSpec((tb * TOP_K,), lambda i: (jnp.minimum(i + 1, steps - 1),), memory_space=pltpu.SMEM),
                  pl.BlockSpec((tb, TOP_K), lambda i: (i, 0)),
                  pl.BlockSpec((tb, d), lambda i: (i, 0)),
                  _mod_specs(g2, tb, d, rows_per_mod, 1),
                  pl.BlockSpec((1, d), lambda i: (0, 0)),
                  pl.BlockSpec(memory_space=pl.ANY)],
        out_specs=pl.BlockSpec((tb, d), lambda i: (i, 0)),
        out_shape=jax.ShapeDtypeStruct((t, d), F32),
        scratch_shapes=[pltpu.VMEM((2, TOP_K, tb, d // 2), jnp.uint32), pltpu.SemaphoreType.DMA((2,))],
        compiler_params=_cparams(("arbitrary",)),
        name="moe_combine",
    )(pos_flat, pos_flat, w8, xs, g2, gf, y_sorted)


def _plan(e_rank_p, cum_p, gate_p, e_rank_s, cum_s, gate_s, n_exp):
    bm = MOE_BM
    sel_p = e_rank_p[:, :n_exp] < TOP_K
    sel_s = e_rank_s[:, :n_exp] < TOP_K
    cnt_p = jnp.sum(sel_p, axis=0, dtype=jnp.int32)
    counts = cnt_p + jnp.sum(sel_s, axis=0, dtype=jnp.int32)
    padded = (counts + bm - 1) // bm * bm
    pad_end = jnp.cumsum(padded)
    pad_start = pad_end - padded
    n_assign = (e_rank_p.shape[0] + e_rank_s.shape[0]) * TOP_K
    n_blocks = -(-n_assign // bm) + n_exp
    blk_start = jnp.arange(n_blocks, dtype=jnp.int32) * bm
    block_e = jnp.minimum(jnp.sum(pad_end[None, :] <= blk_start[:, None], axis=1, dtype=jnp.int32), n_exp - 1)
    block_valid = jnp.clip(pad_start[block_e] + counts[block_e] - blk_start, 0, bm).astype(jnp.int32)
    n_used = pad_end[-1] // bm
    block_idx = jnp.minimum(jnp.arange(n_blocks, dtype=jnp.int32), jnp.maximum(n_used - 1, 0)).astype(jnp.int32)
    slots = jnp.arange(TOP_K, dtype=jnp.int32)[None, :, None]

    def compact(e_rank, cum, gate, offset):
        dest = pad_start[None, :] + offset[None, :] + cum[:, :n_exp]
        hit = e_rank[:, None, :n_exp] == slots
        pos = jnp.sum(jnp.where(hit, dest[:, None, :], 0), axis=-1, dtype=jnp.int32)
        wts = jnp.sum(jnp.where(hit, gate[:, None, :n_exp], 0.0), axis=-1)
        return pos.reshape(-1), wts

    pos_p, w_p = compact(e_rank_p, cum_p, gate_p, jnp.zeros_like(cnt_p))
    pos_s, w_s = compact(e_rank_s, cum_s, gate_s, cnt_p)
    return block_e, block_valid, block_idx, n_blocks * bm, pos_p, w_p, pos_s, w_s


def kernel(x_prompt, x_sample, c_prompt, c_sample, cache_k, cache_v, page_table, state_lru_h, state_conv, w_mod, b_mod, g_norm1, g_norm2, w_in, conv_w, conv_b, w_gate_a, b_gate_a, w_gate_x, b_gate_x, lru_lambda, sb_bias, g_out_lru, g_out_att, w_out, w_router, b_router, w1_e, w3_e, w2_e, w1_s, w3_s, w2_s, g_final):
    nb, s, d = x_prompt.shape
    db = x_sample.shape[0]
    w = conv_w.shape[-1]
    n_exp = w_router.shape[-1]
    l = 0
    row = lambda v: v.reshape(1, -1)

    m = _mod_call(jnp.concatenate([c_prompt, c_sample], axis=0), w_mod[l], row(b_mod[l]))
    mods_p = [m[:nb, i * d:(i + 1) * d].reshape(nb, 1, d) for i in range(6)]
    mods_s = [m[nb:, i * d:(i + 1) * d] for i in range(6)]

    w_in_b = w_in[l].astype(BF16)
    w_out_b = w_out[l].astype(BF16)
    wa_b = w_gate_a[l].astype(BF16)
    wx_b = w_gate_x[l].astype(BF16)
    wr2 = jnp.concatenate([w_router[l], w_router[l]], axis=1)
    wr_hi = wr2.astype(BF16)
    wr_lo = (wr2 - wr_hi.astype(F32)).astype(BF16)
    br2 = row(jnp.concatenate([b_router[l], b_router[l]]).astype(F32))
    w1s_b, w3s_b, w2s_b = w1_s[l].astype(BF16), w3_s[l].astype(BF16), w2_s[l].astype(BF16)
    lru_args = (conv_w[l], row(conv_b[l]), wa_b, row(b_gate_a[l]), wx_b, row(b_gate_x[l]), row(lru_lambda[l]))

    def mix_and_route(yl, ya, x, mods, rows_per_mod, tm):
        return _out_proj_call(yl, ya, x, row(g_out_lru[l]), row(g_out_att[l]), w_out_b, mods[2], mods[4], mods[3],
                              mods[5], row(g_norm2[l]), wr_hi, wr_lo, br2, w1s_b, w3s_b, w2s_b, rows_per_mod, tm)

    xp = x_prompt.reshape(nb * s, d)
    proj_p = _in_proj_call(xp, row(g_norm1[l]), mods_p[1], mods_p[0], w_in_b, s, 512)
    yl_p, hl_p = _lru_prompt_call(proj_p, nb, *lru_args)
    ya_p = _attn_prompt_call(proj_p, nb, sb_bias[l])
    xs_p, h2_p, gate_p, rank_p, cum_p = mix_and_route(yl_p.reshape(nb * s, w), ya_p, xp, mods_p, s, 256)

    xsm = x_sample.reshape(db, d)
    proj_s = _in_proj_call(xsm, row(g_norm1[l]), mods_s[1], mods_s[0], w_in_b, 1, db)
    yl_s, hl_s = _lru_sample_call(proj_s, state_conv[l].reshape(db, (CONV_W - 1) * w), state_lru_h[l], *lru_args)
    ya_s = _attn_sample_call(proj_s[2], cache_k[l], cache_v[l], page_table, sb_bias[l])
    xs_s, h2_s, gate_s, rank_s, cum_s = mix_and_route(yl_s, ya_s, xsm, mods_s, 1, db)

    block_e, block_valid, block_idx, n_rows, pos_p, w8_p, pos_s, w8_s = _plan(
        rank_p, cum_p, gate_p, rank_s, cum_s, gate_s, n_exp)
    x_sorted = _scatter_call(h2_p, pos_p, n_rows)
    x_sorted = _scatter_call(h2_s, pos_s, n_rows, xs_prev=x_sorted, tb=db)
    y_sorted = _experts_call(block_e, block_valid, block_idx, x_sorted, w1_e[l], w3_e[l], w2_e[l])
    y_p = _combine_call(pos_p, w8_p, xs_p, mods_p[5], row(g_final), y_sorted, s)
    y_s = _combine_call(pos_s, w8_s, xs_s, mods_s[5], row(g_final), y_sorted, 1)

    nh = w // HEAD_DIM
    tail = CONV_W - 1
    k_p = proj_p[3].reshape(1, nb, s, nh, HEAD_DIM)
    v_p = proj_p[4].reshape(1, nb, s, nh, HEAD_DIM)
    conv_p = proj_p[0].reshape(nb, s, w)[:, s - tail:, :][None]
    k_s = proj_s[3].reshape(1, db, 1, nh, HEAD_DIM)
    v_s = proj_s[4].reshape(1, db, 1, nh, HEAD_DIM)
    conv_s = jnp.concatenate([state_conv[l][:, 1:, :], proj_s[0][:, None, :]], axis=1)[None]
    return (y_p.reshape(nb, s, d), y_s.reshape(db, 1, d), k_p, v_p, hl_p[None], conv_p,
            k_s, v_s, hl_s[None], conv_s)
```
